```python
import jax, jax.numpy as jnp
from jax import lax
import numpy as np

D_MODEL = 1024
BATCH = 2
SEQ = 16384
DEPTH = 1

CHUNK = 64
D_MIX = D_MODEL
M_HEADS = 4
M_WIDTH = D_MIX // 2
M_HEAD_DIM = M_WIDTH // M_HEADS
QK_CONV = 4
A_HEADS = 8
A_WIDTH = D_MIX - M_WIDTH
A_HEAD_DIM = A_WIDTH // A_HEADS
BAND_CHUNKS = 8
BAND = (BAND_CHUNKS + 1) * CHUNK
MAX_REL = 128
N_REL = 2 * MAX_REL + 1
D_FF = 2816
FFN_CONV = 3
D_PLE = 256
EPS = 1e-6

M_Q = 0
M_K = M_Q + M_WIDTH
M_V = M_K + M_WIDTH
M_O = M_V + M_WIDTH
M_I = M_O + M_WIDTH
M_F = M_I + M_HEADS
A_Q = M_F + M_HEADS
A_K = A_Q + A_WIDTH
A_V = A_K + A_WIDTH
D_IN = A_V + A_WIDTH

kernel_name = 'hymba_mlstm_chunkband_convglu'


def rmsnorm(x, g):
    xf = x.astype(jnp.float32)
    y = xf * lax.rsqrt(jnp.mean(xf * xf, axis=-1, keepdims=True) + EPS)
    return (y * g.astype(jnp.float32)).astype(x.dtype)


def head_layernorm(h, g, n_heads):
    mu = jnp.mean(h, axis=-1, keepdims=True)
    var = jnp.mean(jnp.square(h - mu), axis=-1, keepdims=True)
    y = (h - mu) * lax.rsqrt(var + EPS)
    B, S = h.shape[0], h.shape[1]
    return y.reshape(B, S, -1) * g.astype(jnp.float32)


def causal_dwconv(x, w, b):
    K, C = w.shape
    y = lax.conv_general_dilated(x, w[:, None, :], window_strides=(1,), padding=[(K - 1, 0)],
                                 dimension_numbers=('NWC', 'WIO', 'NWC'), feature_group_count=C)
    return y + b


def mlstm_chunkwise(q, k, v, i_pre, f_pre):
    B, S, NH, DH = q.shape
    NC = S // CHUNK
    L = CHUNK

    def to_chunks(t):
        return t.astype(jnp.float32).reshape(B, NC, L, NH, DH).transpose(1, 0, 3, 2, 4)

    def gate_chunks(t):
        return t.astype(jnp.float32).reshape(B, NC, L, NH).transpose(1, 0, 3, 2)

    qc = to_chunks(q)
    kc = to_chunks(k) * (DH ** -0.5)
    vc = to_chunks(v)
    ic = gate_chunks(i_pre)
    fc = jax.nn.log_sigmoid(gate_chunks(f_pre))
    tril = jnp.tril(jnp.ones((L, L), dtype=bool))

    def step(carry, xs):
        C, n, m = carry
        qb, kb, vb, ib, fb = xs
        b = jnp.cumsum(fb, axis=-1)
        D = jnp.where(tril, b[..., :, None] - b[..., None, :] + ib[..., None, :], -jnp.inf)
        inter = b + m[..., None]
        m_t = jnp.maximum(inter, jnp.max(D, axis=-1))
        scores = jnp.einsum('bhtk,bhsk->bhts', qb, kb) * jnp.exp(D - m_t[..., None])
        w_inter = jnp.exp(inter - m_t)
        num = jnp.einsum('bhts,bhsv->bhtv', scores, vb) + w_inter[..., None] * jnp.einsum('bhvk,bhtk->bhtv', C, qb)
        den = jnp.sum(scores, axis=-1) + w_inter * jnp.einsum('bhtk,bhk->bht', qb, n)
        h = num / jnp.maximum(jnp.abs(den), jnp.exp(-m_t))[..., None]
        bL = b[..., -1]
        g = bL[..., None] - b + ib
        m_new = jnp.maximum(bL + m, jnp.max(g, axis=-1))
        wj = jnp.exp(g - m_new[..., None])
        decay = jnp.exp(bL + m - m_new)
        C_new = decay[..., None, None] * C + jnp.einsum('bhs,bhsv,bhsk->bhvk', wj, vb, kb)
        n_new = decay[..., None] * n + jnp.einsum('bhs,bhsk->bhk', wj, kb)
        return (C_new, n_new, m_new), h

    init = (jnp.zeros((B, NH, DH, DH), jnp.float32), jnp.zeros((B, NH, DH), jnp.float32),
            jnp.zeros((B, NH), jnp.float32))
    _, hs = lax.scan(step, init, (qc, kc, vc, ic, fc))
    return hs.transpose(1, 0, 3, 2, 4).reshape(B, S, NH, DH)


def chunk_band_attention(q, k, v, rel_bias):
    B, S, H, Dh = q.shape
    NC = S // CHUNK
    pad = BAND_CHUNKS * CHUNK
    kp = jnp.pad(k, ((0, 0), (pad, 0), (0, 0), (0, 0)))
    vp = jnp.pad(v, ((0, 0), (pad, 0), (0, 0), (0, 0)))
    qc = q.reshape(B, NC, CHUNK, H, Dh).transpose(1, 0, 2, 3, 4)
    t = jnp.arange(CHUNK)
    r = jnp.arange(BAND)
    dist = pad + t[:, None] - r[None, :]
    bias = rel_bias.astype(jnp.float32)[:, jnp.clip(dist, -MAX_REL, MAX_REL) + MAX_REL]
    scale = Dh ** -0.5

    def one_chunk(args):
        c, qb = args
        kb = lax.dynamic_slice_in_dim(kp, c * CHUNK, BAND, axis=1)
        vb = lax.dynamic_slice_in_dim(vp, c * CHUNK, BAND, axis=1)
        s = jnp.einsum('bthd,brhd->bhtr', qb, kb).astype(jnp.float32) * scale + bias
        valid = r >= pad - c * CHUNK
        s = jnp.where(valid[None, None, None, :], s, -jnp.inf)
        probs = jax.nn.softmax(s, axis=-1).astype(vb.dtype)
        return jnp.einsum('bhtr,brhd->bthd', probs, vb)

    out = lax.map(one_chunk, (jnp.arange(NC), qc))
    return out.transpose(1, 0, 2, 3, 4).reshape(B, S, H * Dh)


def conv_glu_ffn(x, w_up, conv_w, conv_b, w_down):
    u = causal_dwconv(x @ w_up, conv_w, conv_b)
    gate, val = jnp.split(u, 2, axis=-1)
    return (jax.nn.gelu(gate, approximate=True) * val) @ w_down


def setup_inputs(seed: int = 0) -> dict:
    key = jax.random.key(seed)
    ks = jax.random.split(key, 20)

    def nrm(k, shape, scale):
        return jax.random.normal(k, shape, jnp.float32) * scale

    def gain(k, shape):
        return 1.0 + nrm(k, shape, 0.02)

    f_bias = jnp.linspace(3.0, 6.0, M_HEADS, dtype=jnp.float32)[None, :]
    return {
        'x': nrm(ks[0], (BATCH, SEQ, D_MODEL), 1.0),
        'p': nrm(ks[1], (DEPTH, BATCH, SEQ, D_PLE), 1.0),
        'g_mix': gain(ks[2], (DEPTH, D_MODEL)),
        'w_in': nrm(ks[3], (DEPTH, D_MODEL, D_IN), D_MODEL ** -0.5),
        'b_igate': nrm(ks[4], (DEPTH, M_HEADS), 0.1),
        'b_fgate': f_bias + nrm(ks[5], (DEPTH, M_HEADS), 0.1),
        'w_qk_conv': nrm(ks[6], (DEPTH, QK_CONV, 2 * M_WIDTH), QK_CONV ** -0.5),
        'b_qk_conv': nrm(ks[7], (DEPTH, 2 * M_WIDTH), 0.02),
        'g_mhead': gain(ks[8], (DEPTH, M_WIDTH)),
        'rel_bias': nrm(ks[9], (DEPTH, A_HEADS, N_REL), 0.5),
        'w_out': nrm(ks[10], (DEPTH, D_MIX, D_MODEL), D_MIX ** -0.5),
        'g_ffn': gain(ks[11], (DEPTH, D_MODEL)),
        'w_ffn_up': nrm(ks[12], (DEPTH, D_MODEL, 2 * D_FF), D_MODEL ** -0.5),
        'w_ffn_conv': nrm(ks[13], (DEPTH, FFN_CONV, 2 * D_FF), FFN_CONV ** -0.5),
        'b_ffn_conv': nrm(ks[14], (DEPTH, 2 * D_FF), 0.02),
        'w_ffn_down': nrm(ks[15], (DEPTH, D_FF, D_MODEL), D_FF ** -0.5),
        'g_ple': gain(ks[16], (DEPTH, D_MODEL)),
        'w_ple_gate': nrm(ks[17], (DEPTH, D_MODEL, D_MODEL), D_MODEL ** -0.5),
        'w_ple_proj': nrm(ks[18], (DEPTH, D_PLE, D_MODEL), D_PLE ** -0.5),
        'g_final': gain(ks[19], (D_MODEL,)),
    }


def reference(x, p, g_mix, w_in, b_igate, b_fgate, w_qk_conv, b_qk_conv, g_mhead, rel_bias,
              w_out, g_ffn, w_ffn_up, w_ffn_conv, b_ffn_conv, w_ffn_down, g_ple, w_ple_gate,
              w_ple_proj, g_final):
    B, S, _ = x.shape
    h = x
    for i in range(DEPTH):
        a = rmsnorm(h, g_mix[i])
        z = a @ w_in[i]
        qk = jax.nn.silu(causal_dwconv(z[..., M_Q:M_V], w_qk_conv[i], b_qk_conv[i]))
        mq = qk[..., :M_WIDTH].reshape(B, S, M_HEADS, M_HEAD_DIM)
        mk = qk[..., M_WIDTH:].reshape(B, S, M_HEADS, M_HEAD_DIM)
        mv = z[..., M_V:M_O].reshape(B, S, M_HEADS, M_HEAD_DIM)
        mo = z[..., M_O:M_I]
        mi = z[..., M_I:M_F] + b_igate[i]
        mf = z[..., M_F:A_Q] + b_fgate[i]
        mh = mlstm_chunkwise(mq, mk, mv, mi, mf)
        mh = (head_layernorm(mh, g_mhead[i], M_HEADS) * jax.nn.sigmoid(mo.astype(jnp.float32))).astype(x.dtype)
        aq = z[..., A_Q:A_K].reshape(B, S, A_HEADS, A_HEAD_DIM)
        ak = z[..., A_K:A_V].reshape(B, S, A_HEADS, A_HEAD_DIM)
        av = z[..., A_V:D_IN].reshape(B, S, A_HEADS, A_HEAD_DIM)
        ah = chunk_band_attention(aq, ak, av, rel_bias[i])
        h = h + jnp.concatenate([mh, ah], axis=-1) @ w_out[i]
        h = h + conv_glu_ffn(rmsnorm(h, g_ffn[i]), w_ffn_up[i], w_ffn_conv[i], b_ffn_conv[i], w_ffn_down[i])
        gate = jax.nn.sigmoid((rmsnorm(h, g_ple[i]) @ w_ple_gate[i]).astype(jnp.float32)).astype(x.dtype)
        h = h + (p[i] @ w_ple_proj[i]) * gate
    return rmsnorm(h, g_final)
```

```python
import functools

import jax
import jax.numpy as jnp
from jax import lax
from jax.experimental import pallas as pl
from jax.experimental.pallas import tpu as pltpu

F32 = jnp.float32
BF16 = jnp.bfloat16

D_MODEL = 1024
CHUNK = 64
M_HEADS = 4
M_WIDTH = 512
M_HEAD_DIM = 128
QK_CONV = 4
A_HEADS = 8
A_WIDTH = 512
A_HEAD_DIM = 64
BAND_CHUNKS = 8
MAX_REL = 128
D_FF = 2816
FFN_CONV = 3
D_PLE = 256
EPS = 1e-6

LANES = 128
SUBLANES = 8
NEG = -1e30

TM_IN = 512
ML_CHUNK = 128
ML_BLOCK = 1024
TQ = 256
NKB = 1 + (BAND_CHUNKS * CHUNK) // TQ
TM_FF = 512
FC = 256
NCH = D_FF // FC

VMEM_LIMIT = 56 * 1024 * 1024


def _rms(x, g):
    return x * lax.rsqrt(jnp.mean(x * x, axis=-1, keepdims=True) + EPS) * g


def _sigmoid(x):
    return 1.0 / (1.0 + jnp.exp(-x))


def _log_sigmoid(x):
    return jnp.minimum(x, 0.0) - jnp.log1p(jnp.exp(-jnp.abs(x)))


def _const_spec(shape):
    n = len(shape)
    return pl.BlockSpec(shape, lambda *_: (0,) * n, pipeline_mode=pl.Buffered(1))


def _inproj_kernel(x_ref, g_ref, wqk_ref, wvo_ref, wg_ref, wgt_ref, watt_ref, cw_ref, cb_ref,
                   gb_ref, gbt_ref,
                   mq_ref, mk_ref, mv_ref, mo_ref, gcol_ref, grow_ref, aq_ref, ak_ref, av_ref,
                   zext_ref):
    tm = x_ref.shape[0]

    @pl.when(pl.program_id(1) == 0)
    def _():
        zext_ref[0:SUBLANES, :] = jnp.zeros((SUBLANES, 2 * M_WIDTH), F32)

    a = _rms(x_ref[...], g_ref[...]).astype(BF16)

    zqk = jnp.dot(a, wqk_ref[...], preferred_element_type=F32)
    zext_ref[SUBLANES:, :] = zqk
    cw = cw_ref[...]
    acc = cb_ref[...] + cw[QK_CONV - 1:QK_CONV, :] * zqk
    for k in range(1, QK_CONV):
        acc = acc + cw[QK_CONV - 1 - k:QK_CONV - k, :] * zext_ref[SUBLANES - k:SUBLANES - k + tm, :]
    zext_ref[0:SUBLANES, :] = zext_ref[tm:tm + SUBLANES, :]
    qk = acc * _sigmoid(acc)
    mq_ref[...] = qk[:, :M_WIDTH].astype(BF16)
    mk_ref[...] = (qk[:, M_WIDTH:] * (M_HEAD_DIM ** -0.5)).astype(BF16)

    zvo = jnp.dot(a, wvo_ref[...], preferred_element_type=F32)
    mv_ref[...] = zvo[:, :M_WIDTH].astype(BF16)
    mo_ref[...] = zvo[:, M_WIDTH:]

    zg = jnp.dot(a, wg_ref[...], preferred_element_type=F32) + gb_ref[...]
    lane = lax.broadcasted_iota(jnp.int32, zg.shape, 1)
    gcol_ref[...] = jnp.where(lane < M_HEADS, zg, _log_sigmoid(zg))
    zgt = lax.dot_general(wgt_ref[...], a, (((1,), (1,)), ((), ())), preferred_element_type=F32) + gbt_ref[...]
    row = lax.broadcasted_iota(jnp.int32, zgt.shape, 0)
    grow_ref[...] = jnp.where(row < M_HEADS, zgt, _log_sigmoid(zgt))

    zatt = jnp.dot(a, watt_ref[...], preferred_element_type=F32)
    aq_ref[...] = (zatt[:, :A_WIDTH] * (A_HEAD_DIM ** -0.5)).astype(BF16)
    ak_ref[...] = zatt[:, A_WIDTH:2 * A_WIDTH].astype(BF16)
    av_ref[...] = zatt[:, 2 * A_WIDTH:].astype(BF16)


def _inproj(x, g_mix, wqk, wvo, wg, wgt, watt, cw, cb, gb, gbt):
    B, S, D = x.shape
    tm = TM_IN
    tok = lambda w: pl.BlockSpec((None, tm, w), lambda b, i: (b, i, 0))
    out_shape = (
        jax.ShapeDtypeStruct((B, S, M_WIDTH), BF16),
        jax.ShapeDtypeStruct((B, S, M_WIDTH), BF16),
        jax.ShapeDtypeStruct((B, S, M_WIDTH), BF16),
        jax.ShapeDtypeStruct((B, S, M_WIDTH), F32),
        jax.ShapeDtypeStruct((B, S, LANES), F32),
        jax.ShapeDtypeStruct((B, SUBLANES, S), F32),
        jax.ShapeDtypeStruct((B, S, A_WIDTH), BF16),
        jax.ShapeDtypeStruct((B, S, A_WIDTH), BF16),
        jax.ShapeDtypeStruct((B, S, A_WIDTH), BF16),
    )
    out_specs = (tok(M_WIDTH), tok(M_WIDTH), tok(M_WIDTH), tok(M_WIDTH), tok(LANES),
                 pl.BlockSpec((None, SUBLANES, tm), lambda b, i: (b, 0, i)),
                 tok(A_WIDTH), tok(A_WIDTH), tok(A_WIDTH))
    in_specs = [tok(D)] + [_const_spec(w.shape) for w in (g_mix, wqk, wvo, wg, wgt, watt, cw, cb, gb, gbt)]
    return pl.pallas_call(
        _inproj_kernel,
        grid=(B, S // tm),
        in_specs=in_specs,
        out_specs=out_specs,
        out_shape=out_shape,
        scratch_shapes=[pltpu.VMEM((tm + SUBLANES, 2 * M_WIDTH), F32)],
        compiler_params=pltpu.CompilerParams(
            dimension_semantics=("arbitrary", "arbitrary"), vmem_limit_bytes=VMEM_LIMIT),
        name="inproj",
    )(x, g_mix, wqk, wvo, wg, wgt, watt, cw, cb, gb, gbt)


def _mlstm_kernel(q_ref, k_ref, v_ref, o_ref, gcol_ref, grow_ref, gh_ref, out_ref,
                  c_ref, n_ref, m_ref):
    L = ML_CHUNK
    dh = M_HEAD_DIM

    @pl.when(pl.program_id(1) == 0)
    def _():
        c_ref[...] = jnp.zeros(c_ref.shape, F32)
        n_ref[...] = jnp.zeros(n_ref.shape, F32)
        m_ref[...] = jnp.zeros(m_ref.shape, F32)

    rr = lax.broadcasted_iota(jnp.int32, (L, L), 0)
    cc = lax.broadcasted_iota(jnp.int32, (L, L), 1)
    causal = rr >= cc
    lower_f = causal.astype(F32)
    upper_f = (rr <= cc).astype(F32)

    def chunk(c, carry):
        rows = pl.ds(pl.multiple_of(c * L, L), L)
        gc = gcol_ref[rows, :]
        gr = grow_ref[c]
        bc = jnp.dot(lower_f, gc, preferred_element_type=F32, precision=lax.Precision.HIGHEST)
        br = jnp.dot(gr, upper_f, preferred_element_type=F32, precision=lax.Precision.HIGHEST)
        for h in range(M_HEADS):
            hs = slice(h * dh, (h + 1) * dh)
            i_c = gc[:, h:h + 1]
            b_c = bc[:, M_HEADS + h:M_HEADS + h + 1]
            i_r = gr[h:h + 1, :]
            b_r = br[M_HEADS + h:M_HEADS + h + 1, :]
            m_prev = m_ref[h][0:1, 0:1]
            n_prev = n_ref[h][0:1, :]
            c_prev = c_ref[h]

            dmat = jnp.where(causal, b_c - b_r + i_r, -jnp.inf)
            inter = b_c + m_prev
            m_t = jnp.maximum(inter, jnp.max(dmat, axis=-1, keepdims=True))
            qh = q_ref[rows, hs]
            kh = k_ref[rows, hs]
            vh = v_ref[rows, hs]
            s = lax.dot_general(qh, kh, (((1,), (1,)), ((), ())), preferred_element_type=F32)
            sw = s * jnp.exp(dmat - m_t)
            w_inter = jnp.exp(inter - m_t)
            num = (jnp.dot(sw.astype(BF16), vh, preferred_element_type=F32)
                   + w_inter * jnp.dot(qh, c_prev.astype(BF16), preferred_element_type=F32))
            den = (jnp.sum(sw, axis=-1, keepdims=True)
                   + w_inter * jnp.sum(qh.astype(F32) * n_prev, axis=-1, keepdims=True))
            hh = num / jnp.maximum(jnp.abs(den), jnp.exp(-m_t))

            mu = jnp.mean(hh, axis=-1, keepdims=True)
            dc = hh - mu
            var = jnp.mean(dc * dc, axis=-1, keepdims=True)
            y = dc * lax.rsqrt(var + EPS) * gh_ref[:, hs] * _sigmoid(o_ref[rows, hs])
            out_ref[rows, hs] = y.astype(out_ref.dtype)

            b_last = b_r[:, L - 1:L]
            g_r = b_last - b_r + i_r
            m_new = jnp.maximum(b_last + m_prev, jnp.max(g_r, axis=-1, keepdims=True))
            w_c = jnp.exp(b_last - b_c + i_c - m_new)
            decay = jnp.exp(b_last + m_prev - m_new)
            kw = kh.astype(F32) * w_c
            upd = jnp.dot(kw.T.astype(BF16), vh, preferred_element_type=F32)
            c_ref[h] = decay * c_prev + upd
            n_ref[h] = jnp.broadcast_to(decay * n_prev + jnp.sum(kw, axis=0, keepdims=True), (SUBLANES, dh))
            m_ref[h] = jnp.broadcast_to(m_new, (SUBLANES, LANES))
        return carry

    lax.fori_loop(0, q_ref.shape[0] // L, chunk, 0)


def _mlstm(mq, mk, mv, mo, gcol, grow4, g_mhead):
    B, S, W = mq.shape
    tb = ML_BLOCK
    ncb = tb // ML_CHUNK
    tok = lambda w: pl.BlockSpec((None, tb, w), lambda b, j: (b, j, 0))
    return pl.pallas_call(
        _mlstm_kernel,
        grid=(B, S // tb),
        in_specs=[tok(W), tok(W), tok(W), tok(W), tok(LANES),
                  pl.BlockSpec((None, ncb, SUBLANES, ML_CHUNK), lambda b, j: (b, j, 0, 0)),
                  _const_spec(g_mhead.shape)],
        out_specs=tok(W),
        out_shape=jax.ShapeDtypeStruct((B, S, W), BF16),
        scratch_shapes=[pltpu.VMEM((M_HEADS, M_HEAD_DIM, M_HEAD_DIM), F32),
                        pltpu.VMEM((M_HEADS, SUBLANES, M_HEAD_DIM), F32),
                        pltpu.VMEM((M_HEADS, SUBLANES, LANES), F32)],
        compiler_params=pltpu.CompilerParams(
            dimension_semantics=("arbitrary", "arbitrary"), vmem_limit_bytes=VMEM_LIMIT),
        name="mlstm",
    )(mq, mk, mv, mo, gcol, grow4, g_mhead)


def _attn_kernel(q_ref, *refs):
    k_refs = refs[:NKB]
    v_refs = refs[NKB:2 * NKB]
    bias_ref = refs[2 * NKB]
    out_ref = refs[2 * NKB + 1]
    j = pl.program_id(1)
    nk = NKB * TQ
    kpos = lax.broadcasted_iota(jnp.int32, (1, nk), 1)
    valid = kpos >= (NKB - 1 - j) * TQ
    for h in range(A_HEADS):
        hs = slice(h * A_HEAD_DIM, (h + 1) * A_HEAD_DIM)
        qh = q_ref[:, hs]
        kh = jnp.concatenate([r[:, hs] for r in k_refs], axis=0)
        vh = jnp.concatenate([r[:, hs] for r in v_refs], axis=0)
        s = lax.dot_general(qh, kh, (((1,), (1,)), ((), ())), preferred_element_type=F32)
        s = jnp.where(valid, s + bias_ref[h], NEG)
        m = jnp.max(s, axis=-1, keepdims=True)
        p = jnp.exp(s - m)
        l = jnp.sum(p, axis=-1, keepdims=True)
        o = jnp.dot(p.astype(BF16), vh, preferred_element_type=F32) / l
        out_ref[:, hs] = o.astype(out_ref.dtype)


def _band_bias(rel_bias):
    nk = NKB * TQ
    t = jnp.arange(TQ)[:, None]
    r = jnp.arange(nk)[None, :]
    dist = t + (NKB - 1) * TQ - r
    qc = t // CHUNK + (NKB - 1) * TQ // CHUNK
    kc = r // CHUNK
    visible = (kc <= qc) & (kc >= qc - BAND_CHUNKS)
    table = rel_bias.astype(F32)[:, jnp.clip(dist, -MAX_REL, MAX_REL) + MAX_REL]
    return jnp.where(visible[None], table, NEG)


def _attention(aq, ak, av, bias):
    B, S, W = aq.shape
    blk = lambda back: pl.BlockSpec((None, TQ, W), lambda b, j: (b, jnp.maximum(j - back, 0), 0))
    kv_specs = [blk(NKB - 1 - n) for n in range(NKB)]
    return pl.pallas_call(
        _attn_kernel,
        grid=(B, S // TQ),
        in_specs=[blk(0)] + kv_specs + kv_specs + [_const_spec(bias.shape)],
        out_specs=blk(0),
        out_shape=jax.ShapeDtypeStruct((B, S, W), BF16),
        compiler_params=pltpu.CompilerParams(
            dimension_semantics=("arbitrary", "arbitrary"), vmem_limit_bytes=VMEM_LIMIT),
        name="band_attn",
    )(aq, *([ak] * NKB), *([av] * NKB), bias)


def _gelu_tanh(x):
    return 0.5 * x * (1.0 + jnp.tanh(0.7978845608028654 * (x + 0.044715 * (x * x * x))))


def _mixer_kernel(x_ref, mh_ref, ah_ref, p_ref, woa_ref, wob_ref, gffn_ref, wug_ref, wuv_ref,
                  cwg_ref, cwv_ref, cbg_ref, cbv_ref, wd_ref, gple_ref, wpg_ref, wpp_ref, gfin_ref,
                  out_ref, a_ref, acc_ref, extg_ref, extv_ref, carry_ref):
    tm = x_ref.shape[0]

    @pl.when(pl.program_id(1) == 0)
    def _():
        carry_ref[...] = jnp.zeros(carry_ref.shape, F32)

    h1 = (x_ref[...]
          + jnp.dot(mh_ref[...], woa_ref[...], preferred_element_type=F32)
          + jnp.dot(ah_ref[...], wob_ref[...], preferred_element_type=F32))
    acc_ref[...] = h1
    a_ref[...] = _rms(h1, gffn_ref[...]).astype(BF16)

    def conv(ext_ref, u, w, b):
        ext_ref[SUBLANES:, :] = u
        y = b + w[FFN_CONV - 1:FFN_CONV, :] * u
        for k in range(1, FFN_CONV):
            y = y + w[FFN_CONV - 1 - k:FFN_CONV - k, :] * ext_ref[SUBLANES - k:SUBLANES - k + tm, :]
        return y

    def chunk(c, carry):
        a = a_ref[...]
        ug = jnp.dot(a, wug_ref[c], preferred_element_type=F32)
        uv = jnp.dot(a, wuv_ref[c], preferred_element_type=F32)
        extg_ref[0:SUBLANES, :] = carry_ref[c, 0]
        extv_ref[0:SUBLANES, :] = carry_ref[c, 1]
        gate = conv(extg_ref, ug, cwg_ref[c], cbg_ref[c])
        val = conv(extv_ref, uv, cwv_ref[c], cbv_ref[c])
        carry_ref[c, 0] = extg_ref[tm:tm + SUBLANES, :]
        carry_ref[c, 1] = extv_ref[tm:tm + SUBLANES, :]
        act = (_gelu_tanh(gate) * val).astype(BF16)
        acc_ref[...] += jnp.dot(act, wd_ref[c], preferred_element_type=F32)
        return carry

    lax.fori_loop(0, NCH, chunk, 0)

    h2 = acc_ref[...]
    gate = _sigmoid(jnp.dot(_rms(h2, gple_ref[...]).astype(BF16), wpg_ref[...], preferred_element_type=F32))
    emb = jnp.dot(p_ref[...].astype(BF16), wpp_ref[...], preferred_element_type=F32)
    h3 = h2 + emb * gate
    out_ref[...] = _rms(h3, gfin_ref[...])


def _mixer(x, mh, ah, p, weights):
    B, S, D = x.shape
    tm = TM_FF
    tok = lambda w: pl.BlockSpec((None, tm, w), lambda b, i: (b, i, 0))
    return pl.pallas_call(
        _mixer_kernel,
        grid=(B, S // tm),
        in_specs=[tok(D), tok(M_WIDTH), tok(A_WIDTH), tok(D_PLE)] + [_const_spec(w.shape) for w in weights],
        out_specs=tok(D),
        out_shape=jax.ShapeDtypeStruct((B, S, D), F32),
        scratch_shapes=[pltpu.VMEM((tm, D), BF16),
                        pltpu.VMEM((tm, D), F32),
                        pltpu.VMEM((tm + SUBLANES, FC), F32),
                        pltpu.VMEM((tm + SUBLANES, FC), F32),
                        pltpu.VMEM((NCH, 2, SUBLANES, FC), F32)],
        compiler_params=pltpu.CompilerParams(
            dimension_semantics=("arbitrary", "arbitrary"), vmem_limit_bytes=VMEM_LIMIT),
        name="mixer",
    )(x, mh, ah, p, *weights)


def _layer(h, p, g_mix, w_in, b_igate, b_fgate, w_qk_conv, b_qk_conv, g_mhead, rel_bias, w_out,
           g_ffn, w_ffn_up, w_ffn_conv, b_ffn_conv, w_ffn_down, g_ple, w_ple_gate, w_ple_proj, g_out):
    B, S, D = h.shape
    row = lambda v: v.reshape(1, -1).astype(F32)
    m_v = 2 * M_WIDTH
    m_i = 4 * M_WIDTH
    a_q = m_i + 2 * M_HEADS
    wqk = w_in[:, :m_v].astype(BF16)
    wvo = w_in[:, m_v:m_i].astype(BF16)
    wgate = w_in[:, m_i:a_q]
    wg = jnp.pad(wgate, ((0, 0), (0, LANES - 2 * M_HEADS))).astype(BF16)
    wgt = wgate.T.astype(BF16)
    watt = w_in[:, a_q:].astype(BF16)
    gbias = jnp.concatenate([b_igate, b_fgate]).astype(F32)
    gb = jnp.pad(gbias, (0, LANES - 2 * M_HEADS)).reshape(1, LANES)
    gbt = gbias.reshape(2 * M_HEADS, 1)

    mq, mk, mv, mo, gcol, grow, aq, ak, av = _inproj(
        h, row(g_mix), wqk, wvo, wg, wgt, watt, w_qk_conv.astype(F32), row(b_qk_conv), gb, gbt)

    grow4 = grow.reshape(B, 2 * M_HEADS, S // ML_CHUNK, ML_CHUNK).transpose(0, 2, 1, 3)
    mh = _mlstm(mq, mk, mv, mo, gcol, grow4, row(g_mhead))
    ah = _attention(aq, ak, av, _band_bias(rel_bias))

    wup = w_ffn_up.astype(BF16)
    split_cols = lambda w: w.reshape(w.shape[0], NCH, FC).transpose(1, 0, 2)
    weights = (
        w_out[:M_WIDTH].astype(BF16), w_out[M_WIDTH:].astype(BF16), row(g_ffn),
        split_cols(wup[:, :D_FF]), split_cols(wup[:, D_FF:]),
        split_cols(w_ffn_conv[:, :D_FF].astype(F32)), split_cols(w_ffn_conv[:, D_FF:].astype(F32)),
        b_ffn_conv[:D_FF].astype(F32).reshape(NCH, 1, FC), b_ffn_conv[D_FF:].astype(F32).reshape(NCH, 1, FC),
        w_ffn_down.astype(BF16).reshape(NCH, FC, D),
        row(g_ple), w_ple_gate.astype(BF16), w_ple_proj.astype(BF16), g_out,
    )
    return _mixer(h, mh, ah, p, weights)


def kernel(x, p, g_mix, w_in, b_igate, b_fgate, w_qk_conv, b_qk_conv, g_mhead, rel_bias, w_out, g_ffn,
           w_ffn_up, w_ffn_conv, b_ffn_conv, w_ffn_down, g_ple, w_ple_gate, w_ple_proj, g_final):
    depth = w_in.shape[0]
    assert depth == 1, "the final norm is fused into the last layer's channel-mixer kernel"
    i = 0
    return _layer(x, p[i], g_mix[i], w_in[i], b_igate[i], b_fgate[i], w_qk_conv[i], b_qk_conv[i],
                  g_mhead[i], rel_bias[i], w_out[i], g_ffn[i], w_ffn_up[i], w_ffn_conv[i], b_ffn_conv[i],
                  w_ffn_down[i], g_ple[i], w_ple_gate[i], w_ple_proj[i], g_final.reshape(1, -1).astype(F32))
```

```python
import functools

import jax
import jax.numpy as jnp
from jax import lax
from jax.experimental import pallas as pl
from jax.experimental.pallas import tpu as pltpu

F32 = jnp.float32
BF16 = jnp.bfloat16

D_MODEL = 1024
CHUNK = 64
M_HEADS = 4
M_WIDTH = 512
M_HEAD_DIM = 128
QK_CONV = 4
A_HEADS = 8
A_WIDTH = 512
A_HEAD_DIM = 64
BAND_CHUNKS = 8
MAX_REL = 128
D_FF = 2816
FFN_CONV = 3
D_PLE = 256
EPS = 1e-6

LANES = 128
SUBLANES = 8
NEG = -1e30

TM_IN = 512
ML_CHUNK = 128
ML_BLOCK = 1024
TQ = 256
NKB = 1 + (BAND_CHUNKS * CHUNK) // TQ
TM_FF = 512
FC = 256
NCH = D_FF // FC

VMEM_LIMIT = 56 * 1024 * 1024


def _rms(x, g):
    return x * lax.rsqrt(jnp.mean(x * x, axis=-1, keepdims=True) + EPS) * g


def _sigmoid(x):
    return 1.0 / (1.0 + jnp.exp(-x))


def _log_sigmoid(x):
    return jnp.minimum(x, 0.0) - jnp.log1p(jnp.exp(-jnp.abs(x)))


def _const_spec(shape):
    n = len(shape)
    return pl.BlockSpec(shape, lambda *_: (0,) * n, pipeline_mode=pl.Buffered(1))


def _inproj_kernel(x_ref, g_ref, wqk_ref, wvo_ref, wg_ref, wgt_ref, watt_ref, cw_ref, cb_ref,
                   gb_ref, gbt_ref,
                   mq_ref, mk_ref, mv_ref, mo_ref, gcol_ref, grow_ref, aq_ref, ak_ref, av_ref,
                   zext_ref):
    tm = x_ref.shape[0]

    @pl.when(pl.program_id(1) == 0)
    def _():
        zext_ref[0:SUBLANES, :] = jnp.zeros((SUBLANES, 2 * M_WIDTH), F32)

    a = _rms(x_ref[...], g_ref[...]).astype(BF16)

    zqk = jnp.dot(a, wqk_ref[...], preferred_element_type=F32)
    zext_ref[SUBLANES:, :] = zqk
    cw = cw_ref[...]
    acc = cb_ref[...] + cw[QK_CONV - 1:QK_CONV, :] * zqk
    for k in range(1, QK_CONV):
        acc = acc + cw[QK_CONV - 1 - k:QK_CONV - k, :] * zext_ref[SUBLANES - k:SUBLANES - k + tm, :]
    zext_ref[0:SUBLANES, :] = zext_ref[tm:tm + SUBLANES, :]
    qk = acc * _sigmoid(acc)
    mq_ref[...] = qk[:, :M_WIDTH].astype(BF16)
    mk_ref[...] = (qk[:, M_WIDTH:] * (M_HEAD_DIM ** -0.5)).astype(BF16)

    zvo = jnp.dot(a, wvo_ref[...], preferred_element_type=F32)
    mv_ref[...] = zvo[:, :M_WIDTH].astype(BF16)
    mo_ref[...] = zvo[:, M_WIDTH:]

    zg = jnp.dot(a, wg_ref[...], preferred_element_type=F32) + gb_ref[...]
    lane = lax.broadcasted_iota(jnp.int32, zg.shape, 1)
    gcol_ref[...] = jnp.where(lane < M_HEADS, zg, _log_sigmoid(zg))
    zgt = lax.dot_general(wgt_ref[...], a, (((1,), (1,)), ((), ())), preferred_element_type=F32) + gbt_ref[...]
    row = lax.broadcasted_iota(jnp.int32, zgt.shape, 0)
    grow_ref[...] = jnp.where(row < M_HEADS, zgt, _log_sigmoid(zgt))

    zatt = jnp.dot(a, watt_ref[...], preferred_element_type=F32)
    aq_ref[...] = (zatt[:, :A_WIDTH] * (A_HEAD_DIM ** -0.5)).astype(BF16)
    ak_ref[...] = zatt[:, A_WIDTH:2 * A_WIDTH].astype(BF16)
    av_ref[...] = zatt[:, 2 * A_WIDTH:].astype(BF16)


def _inproj(x, g_mix, wqk, wvo, wg, wgt, watt, cw, cb, gb, gbt):
    B, S, D = x.shape
    tm = TM_IN
    tok = lambda w: pl.BlockSpec((None, tm, w), lambda b, i: (b, i, 0))
    out_shape = (
        jax.ShapeDtypeStruct((B, S, M_WIDTH), BF16),
        jax.ShapeDtypeStruct((B, S, M_WIDTH), BF16),
        jax.ShapeDtypeStruct((B, S, M_WIDTH), BF16),
        jax.ShapeDtypeStruct((B, S, M_WIDTH), F32),
        jax.ShapeDtypeStruct((B, S, LANES), F32),
        jax.ShapeDtypeStruct((B, SUBLANES, S), F32),
        jax.ShapeDtypeStruct((B, S, A_WIDTH), BF16),
        jax.ShapeDtypeStruct((B, S, A_WIDTH), BF16),
        jax.ShapeDtypeStruct((B, S, A_WIDTH), BF16),
    )
    out_specs = (tok(M_WIDTH), tok(M_WIDTH), tok(M_WIDTH), tok(M_WIDTH), tok(LANES),
                 pl.BlockSpec((None, SUBLANES, tm), lambda b, i: (b, 0, i)),
                 tok(A_WIDTH), tok(A_WIDTH), tok(A_WIDTH))
    in_specs = [tok(D)] + [_const_spec(w.shape) for w in (g_mix, wqk, wvo, wg, wgt, watt, cw, cb, gb, gbt)]
    return pl.pallas_call(
        _inproj_kernel,
        grid=(B, S // tm),
        in_specs=in_specs,
        out_specs=out_specs,
        out_shape=out_shape,
        scratch_shapes=[pltpu.VMEM((tm + SUBLANES, 2 * M_WIDTH), F32)],
        compiler_params=pltpu.CompilerParams(
            dimension_semantics=("arbitrary", "arbitrary"), vmem_limit_bytes=VMEM_LIMIT),
        name="inproj",
    )(x, g_mix, wqk, wvo, wg, wgt, watt, cw, cb, gb, gbt)


def _mlstm_kernel(q_ref, k_ref, v_ref, o_ref, gcol_ref, grow_ref, gh_ref, out_ref,
                  c_ref, n_ref, m_ref):
    L = ML_CHUNK
    dh = M_HEAD_DIM

    @pl.when(pl.program_id(1) == 0)
    def _():
        c_ref[...] = jnp.zeros(c_ref.shape, F32)
        n_ref[...] = jnp.zeros(n_ref.shape, F32)
        m_ref[...] = jnp.zeros(m_ref.shape, F32)

    rr = lax.broadcasted_iota(jnp.int32, (L, L), 0)
    cc = lax.broadcasted_iota(jnp.int32, (L, L), 1)
    causal = rr >= cc
    lower_f = causal.astype(F32)
    upper_f = (rr <= cc).astype(F32)

    def chunk(c, carry):
        rows = pl.ds(pl.multiple_of(c * L, L), L)
        gc = gcol_ref[rows, :]
        gr = grow_ref[c]
        bc = jnp.dot(lower_f, gc, preferred_element_type=F32, precision=lax.Precision.HIGHEST)
        br = jnp.dot(gr, upper_f, preferred_element_type=F32, precision=lax.Precision.HIGHEST)
        for h in range(M_HEADS):
            hs = slice(h * dh, (h + 1) * dh)
            i_c = gc[:, h:h + 1]
            b_c = bc[:, M_HEADS + h:M_HEADS + h + 1]
            i_r = gr[h:h + 1, :]
            b_r = br[M_HEADS + h:M_HEADS + h + 1, :]
            m_prev = m_ref[h][0:1, 0:1]
            n_prev = n_ref[h][0:1, :]
            c_prev = c_ref[h]

            dmat = jnp.where(causal, b_c - b_r + i_r, -jnp.inf)
            inter = b_c + m_prev
            m_t = jnp.maximum(inter, jnp.max(dmat, axis=-1, keepdims=True))
            qh = q_ref[rows, hs]
            kh = k_ref[rows, hs]
            vh = v_ref[rows, hs]
            s = lax.dot_general(qh, kh, (((1,), (1,)), ((), ())), preferred_element_type=F32)
            sw = s * jnp.exp(dmat - m_t)
            w_inter = jnp.exp(inter - m_t)
            num = (jnp.dot(sw.astype(BF16), vh, preferred_element_type=F32)
                   + w_inter * jnp.dot(qh, c_prev.astype(BF16), preferred_element_type=F32))
            den = (jnp.sum(sw, axis=-1, keepdims=True)
                   + w_inter * jnp.sum(qh.astype(F32) * n_prev, axis=-1, keepdims=True))
            hh = num / jnp.maximum(jnp.abs(den), jnp.exp(-m_t))

            mu = jnp.mean(hh, axis=-1, keepdims=True)
            dc = hh - mu
            var = jnp.mean(dc * dc, axis=-1, keepdims=True)
            y = dc * lax.rsqrt(var + EPS) * gh_ref[:, hs] * _sigmoid(o_ref[rows, hs])
            out_ref[rows, hs] = y.astype(out_ref.dtype)

            b_last = b_r[:, L - 1:L]
            g_r = b_last - b_r + i_r
            m_new = jnp.maximum(b_last + m_prev, jnp.max(g_r, axis=-1, keepdims=True))
            w_c = jnp.exp(b_last - b_c + i_c - m_new)
            decay = jnp.exp(b_last + m_prev - m_new)
            kw = kh.astype(F32) * w_c
            upd = jnp.dot(kw.T.astype(BF16), vh, preferred_element_type=F32)
            c_ref[h] = decay * c_prev + upd
            n_ref[h] = jnp.broadcast_to(decay * n_prev + jnp.sum(kw, axis=0, keepdims=True), (SUBLANES, dh))
            m_ref[h] = jnp.broadcast_to(m_new, (SUBLANES, LANES))
        return carry

    lax.fori_loop(0, q_ref.shape[0] // L, chunk, 0)


def _mlstm(mq, mk, mv, mo, gcol, grow4, g_mhead):
    B, S, W = mq.shape
    tb = ML_BLOCK
    ncb = tb // ML_CHUNK
    tok = lambda w: pl.BlockSpec((None, tb, w), lambda b, j: (b, j, 0))
    return pl.pallas_call(
        _mlstm_kernel,
        grid=(B, S // tb),
        in_specs=[tok(W), tok(W), tok(W), tok(W), tok(LANES),
                  pl.BlockSpec((None, ncb, SUBLANES, ML_CHUNK), lambda b, j: (b, j, 0, 0)),
                  _const_spec(g_mhead.shape)],
        out_specs=tok(W),
        out_shape=jax.ShapeDtypeStruct((B, S, W), BF16),
        scratch_shapes=[pltpu.VMEM((M_HEADS, M_HEAD_DIM, M_HEAD_DIM), F32),
                        pltpu.VMEM((M_HEADS, SUBLANES, M_HEAD_DIM), F32),
                        pltpu.VMEM((M_HEADS, SUBLANES, LANES), F32)],
        compiler_params=pltpu.CompilerParams(
            dimension_semantics=("arbitrary", "arbitrary"), vmem_limit_bytes=VMEM_LIMIT),
        name="mlstm",
    )(mq, mk, mv, mo, gcol, grow4, g_mhead)


def _build_band_bias(rb_ref, bias_ref):
    T = LANES
    rr = lax.broadcasted_iota(jnp.int32, (T, T), 0)
    cc = lax.broadcasted_iota(jnp.int32, (T, T), 1)
    diff = rr - cc
    back = (NKB - 1) * TQ
    for h in range(A_HEADS):
        toeplitz = {}

        def tile_bias(delta):
            if T * delta - (T - 1) >= MAX_REL:
                return jnp.full((T, T), rb_ref[h, 2 * MAX_REL], F32)
            if T * delta + (T - 1) <= -MAX_REL:
                return jnp.full((T, T), rb_ref[h, 0], F32)
            if delta not in toeplitz:
                def body(u, tab):
                    d = u - (T - 1)
                    idx = jnp.clip(T * delta + d, -MAX_REL, MAX_REL) + MAX_REL
                    return jnp.where(diff == d, rb_ref[h, idx], tab)
                toeplitz[delta] = lax.fori_loop(0, 2 * T - 1, body, jnp.zeros((T, T), F32))
            return toeplitz[delta]

        for qi in range(TQ // T):
            for ki in range(NKB * TQ // T):
                qc = (rr + qi * T + back) // CHUNK
                kc = (cc + ki * T) // CHUNK
                visible = (kc <= qc) & (kc >= qc - BAND_CHUNKS)
                q_lo, q_hi = (qi * T + back) // CHUNK, (qi * T + T - 1 + back) // CHUNK
                k_lo, k_hi = (ki * T) // CHUNK, (ki * T + T - 1) // CHUNK
                if k_lo > q_hi or k_hi < q_lo - BAND_CHUNKS:
                    tile = jnp.full((T, T), NEG, F32)
                else:
                    tile = jnp.where(visible, tile_bias(qi - ki + back // T), NEG)
                bias_ref[h, qi * T:(qi + 1) * T, ki * T:(ki + 1) * T] = tile


def _attn_kernel(rb_ref, q_ref, *refs):
    k_refs = refs[:NKB]
    v_refs = refs[NKB:2 * NKB]
    out_ref = refs[2 * NKB]
    bias_ref = refs[2 * NKB + 1]
    j = pl.program_id(1)
    nk = NKB * TQ

    @pl.when((pl.program_id(0) == 0) & (j == 0))
    def _():
        _build_band_bias(rb_ref, bias_ref)

    def heads(at_start):
        kpos = lax.broadcasted_iota(jnp.int32, (1, nk), 1)
        valid = kpos >= (NKB - 1 - j) * TQ
        for h in range(A_HEADS):
            hs = slice(h * A_HEAD_DIM, (h + 1) * A_HEAD_DIM)
            qh = q_ref[:, hs]
            kh = jnp.concatenate([r[:, hs] for r in k_refs], axis=0)
            vh = jnp.concatenate([r[:, hs] for r in v_refs], axis=0)
            s = lax.dot_general(qh, kh, (((1,), (1,)), ((), ())), preferred_element_type=F32)
            s = s + bias_ref[h]
            if at_start:
                s = jnp.where(valid, s, NEG)
            m = jnp.max(s, axis=-1, keepdims=True)
            p = jnp.exp(s - m)
            l = jnp.sum(p, axis=-1, keepdims=True)
            o = jnp.dot(p.astype(BF16), vh, preferred_element_type=F32) / l
            out_ref[:, hs] = o.astype(out_ref.dtype)

    pl.when(j < NKB - 1)(functools.partial(heads, True))
    pl.when(j >= NKB - 1)(functools.partial(heads, False))


def _attention(aq, ak, av, rel_bias):
    B, S, W = aq.shape
    blk = lambda back: pl.BlockSpec((None, TQ, W), lambda b, j: (b, jnp.maximum(j - back, 0), 0))
    kv_specs = [blk(NKB - 1 - n) for n in range(NKB)]
    return pl.pallas_call(
        _attn_kernel,
        grid=(B, S // TQ),
        in_specs=[pl.BlockSpec(memory_space=pltpu.SMEM), blk(0)] + kv_specs + kv_specs,
        out_specs=blk(0),
        out_shape=jax.ShapeDtypeStruct((B, S, W), BF16),
        scratch_shapes=[pltpu.VMEM((A_HEADS, TQ, NKB * TQ), F32)],
        compiler_params=pltpu.CompilerParams(
            dimension_semantics=("arbitrary", "arbitrary"), vmem_limit_bytes=VMEM_LIMIT),
        name="band_attn",
    )(rel_bias.astype(F32), aq, *([ak] * NKB), *([av] * NKB))


def _gelu_tanh(x):
    return 0.5 * x * (1.0 + jnp.tanh(0.7978845608028654 * (x + 0.044715 * (x * x * x))))


def _mixer_kernel(x_ref, mh_ref, ah_ref, p_ref, woa_ref, wob_ref, gffn_ref, wug_ref, wuv_ref,
                  cwg_ref, cwv_ref, cbg_ref, cbv_ref, wd_ref, gple_ref, wpg_ref, wpp_ref, gfin_ref,
                  out_ref, a_ref, acc_ref, u0_ref, u1_ref, carry_ref):
    tm = x_ref.shape[0]

    @pl.when(pl.program_id(1) == 0)
    def _():
        carry_ref[...] = jnp.zeros(carry_ref.shape, F32)

    h1 = (x_ref[...]
          + jnp.dot(mh_ref[...], woa_ref[...], preferred_element_type=F32)
          + jnp.dot(ah_ref[...], wob_ref[...], preferred_element_type=F32))
    acc_ref[...] = h1
    a_ref[...] = _rms(h1, gffn_ref[...]).astype(BF16)

    def up(c, u_ref):
        a = a_ref[...]
        u_ref[0, 0:SUBLANES, :] = carry_ref[c, 0]
        u_ref[1, 0:SUBLANES, :] = carry_ref[c, 1]
        u_ref[0, SUBLANES:, :] = jnp.dot(a, wug_ref[c], preferred_element_type=F32)
        u_ref[1, SUBLANES:, :] = jnp.dot(a, wuv_ref[c], preferred_element_type=F32)

    def conv(u_ref, part, w, b):
        y = b + w[FFN_CONV - 1:FFN_CONV, :] * u_ref[part, SUBLANES:, :]
        for k in range(1, FFN_CONV):
            y = y + w[FFN_CONV - 1 - k:FFN_CONV - k, :] * u_ref[part, SUBLANES - k:SUBLANES - k + tm, :]
        return y

    def down(c, u_ref):
        gate = conv(u_ref, 0, cwg_ref[c], cbg_ref[c])
        val = conv(u_ref, 1, cwv_ref[c], cbv_ref[c])
        carry_ref[c, 0] = u_ref[0, tm:tm + SUBLANES, :]
        carry_ref[c, 1] = u_ref[1, tm:tm + SUBLANES, :]
        act = (_gelu_tanh(gate) * val).astype(BF16)
        acc_ref[...] += jnp.dot(act, wd_ref[c], preferred_element_type=F32)

    assert NCH % 2 == 1
    up(0, u0_ref)

    def body(i, carry):
        c = 2 * i
        up(c + 1, u1_ref)
        down(c, u0_ref)
        up(c + 2, u0_ref)
        down(c + 1, u1_ref)
        return carry

    lax.fori_loop(0, NCH // 2, body, 0)
    down(NCH - 1, u0_ref)

    h2 = acc_ref[...]
    gate = _sigmoid(jnp.dot(_rms(h2, gple_ref[...]).astype(BF16), wpg_ref[...], preferred_element_type=F32))
    emb = jnp.dot(p_ref[...].astype(BF16), wpp_ref[...], preferred_element_type=F32)
    h3 = h2 + emb * gate
    out_ref[...] = _rms(h3, gfin_ref[...])


def _mixer(x, mh, ah, p, weights):
    B, S, D = x.shape
    tm = TM_FF
    tok = lambda w: pl.BlockSpec((None, tm, w), lambda b, i: (b, i, 0))
    return pl.pallas_call(
        _mixer_kernel,
        grid=(B, S // tm),
        in_specs=[tok(D), tok(M_WIDTH), tok(A_WIDTH), tok(D_PLE)] + [_const_spec(w.shape) for w in weights],
        out_specs=tok(D),
        out_shape=jax.ShapeDtypeStruct((B, S, D), F32),
        scratch_shapes=[pltpu.VMEM((tm, D), BF16),
                        pltpu.VMEM((tm, D), F32),
                        pltpu.VMEM((2, tm + SUBLANES, FC), F32),
                        pltpu.VMEM((2, tm + SUBLANES, FC), F32),
                        pltpu.VMEM((NCH, 2, SUBLANES, FC), F32)],
        compiler_params=pltpu.CompilerParams(
            dimension_semantics=("arbitrary", "arbitrary"), vmem_limit_bytes=VMEM_LIMIT),
        name="mixer",
    )(x, mh, ah, p, *weights)


def _layer(h, p, g_mix, w_in, b_igate, b_fgate, w_qk_conv, b_qk_conv, g_mhead, rel_bias, w_out,
           g_ffn, w_ffn_up, w_ffn_conv, b_ffn_conv, w_ffn_down, g_ple, w_ple_gate, w_ple_proj, g_out):
    B, S, D = h.shape
    row = lambda v: v.reshape(1, -1).astype(F32)
    m_v = 2 * M_WIDTH
    m_i = 4 * M_WIDTH
    a_q = m_i + 2 * M_HEADS
    wqk = w_in[:, :m_v].astype(BF16)
    wvo = w_in[:, m_v:m_i].astype(BF16)
    wgate = w_in[:, m_i:a_q]
    wg = jnp.pad(wgate, ((0, 0), (0, LANES - 2 * M_HEADS))).astype(BF16)
    wgt = wgate.T.astype(BF16)
    watt = w_in[:, a_q:].astype(BF16)
    gbias = jnp.concatenate([b_igate, b_fgate]).astype(F32)
    gb = jnp.pad(gbias, (0, LANES - 2 * M_HEADS)).reshape(1, LANES)
    gbt = gbias.reshape(2 * M_HEADS, 1)

    mq, mk, mv, mo, gcol, grow, aq, ak, av = _inproj(
        h, row(g_mix), wqk, wvo, wg, wgt, watt, w_qk_conv.astype(F32), row(b_qk_conv), gb, gbt)

    grow4 = grow.reshape(B, 2 * M_HEADS, S // ML_CHUNK, ML_CHUNK).transpose(0, 2, 1, 3)
    mh = _mlstm(mq, mk, mv, mo, gcol, grow4, row(g_mhead))
    ah = _attention(aq, ak, av, rel_bias)

    wup = w_ffn_up.astype(BF16)
    split_cols = lambda w: w.reshape(w.shape[0], NCH, FC).transpose(1, 0, 2)
    weights = (
        w_out[:M_WIDTH].astype(BF16), w_out[M_WIDTH:].astype(BF16), row(g_ffn),
        split_cols(wup[:, :D_FF]), split_cols(wup[:, D_FF:]),
        split_cols(w_ffn_conv[:, :D_FF].astype(F32)), split_cols(w_ffn_conv[:, D_FF:].astype(F32)),
        b_ffn_conv[:D_FF].astype(F32).reshape(NCH, 1, FC), b_ffn_conv[D_FF:].astype(F32).reshape(NCH, 1, FC),
        w_ffn_down.astype(BF16).reshape(NCH, FC, D),
        row(g_ple), w_ple_gate.astype(BF16), w_ple_proj.astype(BF16), g_out,
    )
    return _mixer(h, mh, ah, p, weights)


def kernel(x, p, g_mix, w_in, b_igate, b_fgate, w_qk_conv, b_qk_conv, g_mhead, rel_bias, w_out, g_ffn,
           w_ffn_up, w_ffn_conv, b_ffn_conv, w_ffn_down, g_ple, w_ple_gate, w_ple_proj, g_final):
    depth = w_in.shape[0]
    assert depth == 1, "the final norm is fused into the last layer's channel-mixer kernel"
    i = 0
    return _layer(x, p[i], g_mix[i], w_in[i], b_igate[i], b_fgate[i], w_qk_conv[i], b_qk_conv[i],
                  g_mhead[i], rel_bias[i], w_out[i], g_ffn[i], w_ffn_up[i], w_ffn_conv[i], b_ffn_conv[i],
                  w_ffn_down[i], g_ple[i], w_ple_gate[i], w_ple_proj[i], g_final.reshape(1, -1).astype(F32))
```

```python
import functools

import numpy as np
import jax
import jax.numpy as jnp
from jax import lax
from jax.experimental import pallas as pl
from jax.experimental.pallas import tpu as pltpu

F32 = jnp.float32
BF16 = jnp.bfloat16

D_MODEL = 1024
CHUNK = 64
M_HEADS = 4
M_WIDTH = 512
M_HEAD_DIM = 128
QK_CONV = 4
A_HEADS = 8
A_WIDTH = 512
A_HEAD_DIM = 64
BAND_CHUNKS = 8
MAX_REL = 128
D_FF = 2816
FFN_CONV = 3
D_PLE = 256
EPS = 1e-6

LANES = 128
SUBLANES = 8
NEG = -1e30
LOG2E = 1.4426950408889634

TM_IN = 512
ML_CHUNK = 128
ML_BLOCK = 512
TQ = 256
NKB = 1 + (BAND_CHUNKS * CHUNK) // TQ
TM_FF = 512
FC = 256
NCH = D_FF // FC

VMEM_LIMIT = 56 * 1024 * 1024

_NT = (((1,), (1,)), ((), ()))


def _rms(x, g):
    return x * lax.rsqrt(jnp.mean(x * x, axis=-1, keepdims=True) + EPS) * g


def _sigmoid(x):
    return 1.0 / (1.0 + jnp.exp(-x))


def _log_sigmoid(x):
    return jnp.minimum(x, 0.0) - jnp.log1p(jnp.exp(-jnp.abs(x)))


def _split3(x):
    hi = x.astype(BF16)
    r = x - hi.astype(F32)
    mid = r.astype(BF16)
    lo = (r - mid.astype(F32)).astype(BF16)
    return hi, mid, lo


def _const_spec(shape):
    n = len(shape)
    return pl.BlockSpec(shape, lambda *_: (0,) * n, pipeline_mode=pl.Buffered(1))


def _inproj_kernel(x_ref, g_ref, wqk_ref, wvot_ref, wg_ref, wgt_ref, watt_ref, cw_ref, cb_ref,
                   gb_ref, gbt_ref,
                   mq_ref, mk_ref, mvt_ref, mot_ref, gcol_ref, grow_ref, aq_ref, ak_ref, av_ref,
                   zext_ref):
    tm = x_ref.shape[0]

    @pl.when(pl.program_id(1) == 0)
    def _():
        zext_ref[0:SUBLANES, :] = jnp.zeros((SUBLANES, 2 * M_WIDTH), F32)

    a = _rms(x_ref[...], g_ref[...]).astype(BF16)

    zqk = jnp.dot(a, wqk_ref[...], preferred_element_type=F32)
    zext_ref[SUBLANES:, :] = zqk
    cw = cw_ref[...]
    acc = cb_ref[...] + cw[QK_CONV - 1:QK_CONV, :] * zqk
    for k in range(1, QK_CONV):
        acc = acc + cw[QK_CONV - 1 - k:QK_CONV - k, :] * zext_ref[SUBLANES - k:SUBLANES - k + tm, :]
    zext_ref[0:SUBLANES, :] = zext_ref[tm:tm + SUBLANES, :]
    qk = acc * _sigmoid(acc)
    mq_ref[...] = qk[:, :M_WIDTH].astype(BF16)
    mk_ref[...] = (qk[:, M_WIDTH:] * (M_HEAD_DIM ** -0.5)).astype(BF16)

    zvot = lax.dot_general(wvot_ref[...], a, _NT, preferred_element_type=F32)
    mvt_ref[...] = zvot[:M_WIDTH].astype(BF16)
    mot_ref[...] = zvot[M_WIDTH:]

    zg = jnp.dot(a, wg_ref[...], preferred_element_type=F32) + gb_ref[...]
    lane = lax.broadcasted_iota(jnp.int32, zg.shape, 1)
    gcol_ref[...] = jnp.where(lane < M_HEADS, zg, _log_sigmoid(zg))
    zgt = lax.dot_general(wgt_ref[...], a, _NT, preferred_element_type=F32) + gbt_ref[...]
    row = lax.broadcasted_iota(jnp.int32, zgt.shape, 0)
    grow_ref[...] = jnp.where(row < M_HEADS, zgt, _log_sigmoid(zgt))

    zatt = jnp.dot(a, watt_ref[...], preferred_element_type=F32)
    aq_ref[...] = (zatt[:, :A_WIDTH] * (LOG2E * A_HEAD_DIM ** -0.5)).astype(BF16)
    ak_ref[...] = zatt[:, A_WIDTH:2 * A_WIDTH].astype(BF16)
    av_ref[...] = zatt[:, 2 * A_WIDTH:].astype(BF16)


def _inproj(x, g_mix, wqk, wvot, wg, wgt, watt, cw, cb, gb, gbt):
    B, S, D = x.shape
    tm = TM_IN
    tok = lambda w: pl.BlockSpec((None, tm, w), lambda b, i: (b, i, 0))
    feat = lambda w: pl.BlockSpec((None, w, tm), lambda b, i: (b, 0, i))
    out_shape = (
        jax.ShapeDtypeStruct((B, S, M_WIDTH), BF16),
        jax.ShapeDtypeStruct((B, S, M_WIDTH), BF16),
        jax.ShapeDtypeStruct((B, M_WIDTH, S), BF16),
        jax.ShapeDtypeStruct((B, M_WIDTH, S), F32),
        jax.ShapeDtypeStruct((B, S, LANES), F32),
        jax.ShapeDtypeStruct((B, SUBLANES, S), F32),
        jax.ShapeDtypeStruct((B, S, A_WIDTH), BF16),
        jax.ShapeDtypeStruct((B, S, A_WIDTH), BF16),
        jax.ShapeDtypeStruct((B, S, A_WIDTH), BF16),
    )
    out_specs = (tok(M_WIDTH), tok(M_WIDTH), feat(M_WIDTH), feat(M_WIDTH), tok(LANES), feat(SUBLANES),
                 tok(A_WIDTH), tok(A_WIDTH), tok(A_WIDTH))
    consts = (g_mix, wqk, wvot, wg, wgt, watt, cw, cb, gb, gbt)
    return pl.pallas_call(
        _inproj_kernel,
        grid=(B, S // tm),
        in_specs=[tok(D)] + [_const_spec(w.shape) for w in consts],
        out_specs=out_specs,
        out_shape=out_shape,
        scratch_shapes=[pltpu.VMEM((tm + SUBLANES, 2 * M_WIDTH), F32)],
        compiler_params=pltpu.CompilerParams(
            dimension_semantics=("arbitrary", "arbitrary"), vmem_limit_bytes=VMEM_LIMIT),
        name="inproj",
    )(x, *consts)


def _mlstm_consts():
    L = ML_CHUNK
    s = np.arange(L)
    lower = (s[:, None] >= s[None, :]).astype(np.float32)
    upper = lower.T
    sel = np.zeros((LANES, M_HEADS * LANES), np.float32)
    for h in range(M_HEADS):
        sel[h, h * LANES:(h + 1) * LANES] = 1.0
        sel[M_HEADS + h, h * LANES:(h + 1) * LANES] = -1.0
    to_bf16 = lambda m: jnp.asarray(m, BF16)
    return (to_bf16(np.concatenate([lower] * 3, axis=1)),
            to_bf16(np.concatenate([upper] * 3, axis=0)),
            to_bf16(np.concatenate([sel] * 3, axis=0)))


def _mlstm_kernel(q_ref, k_ref, vt_ref, ot_ref, gcol_ref, grow_ref, lower3_ref, upper3_ref, sel3_ref, gh_ref,
                  out_ref, c_ref, n_ref, m_ref):
    L = ML_CHUNK
    dh = M_HEAD_DIM

    @pl.when(pl.program_id(1) == 0)
    def _():
        c_ref[...] = jnp.zeros(c_ref.shape, F32)
        n_ref[...] = jnp.zeros(n_ref.shape, F32)
        m_ref[...] = jnp.zeros(m_ref.shape, F32)

    rr = lax.broadcasted_iota(jnp.int32, (L, L), 0)
    cc = lax.broadcasted_iota(jnp.int32, (L, L), 1)
    causal_t = rr <= cc
    lane = lax.broadcasted_iota(jnp.int32, (L, LANES), 1)

    for c in range(q_ref.shape[0] // L):
        rows = slice(c * L, (c + 1) * L)
        gc = gcol_ref[rows, :]
        gr = grow_ref[:, rows]
        bc = jnp.dot(lower3_ref[...], jnp.concatenate(_split3(gc), axis=0), preferred_element_type=F32)
        br = jnp.dot(jnp.concatenate(_split3(gr), axis=1), upper3_ref[...], preferred_element_type=F32)
        ib = jnp.where(lane < M_HEADS, gc, bc)
        a_all = jnp.dot(jnp.concatenate(_split3(ib), axis=1), sel3_ref[...], preferred_element_type=F32)
        for h in range(M_HEADS):
            hs = slice(h * dh, (h + 1) * dh)
            a_c = a_all[:, hs]
            i_r = gr[h:h + 1, :]
            b_r = br[M_HEADS + h:M_HEADS + h + 1, :]
            b_last = b_r[:, L - 1:L]
            m_prev = m_ref[h][0:1, 0:1]
            n_prev = n_ref[h]
            c_prev = c_ref[h]

            dmat = jnp.where(causal_t, a_c + b_r, -jnp.inf)
            inter = b_r + m_prev
            m_t = jnp.maximum(inter, jnp.max(dmat, axis=0, keepdims=True))
            qh = q_ref[rows, hs]
            kh = k_ref[rows, hs]
            vt = vt_ref[hs, rows]
            st = lax.dot_general(kh, qh, _NT, preferred_element_type=F32)
            sw = st * jnp.exp(dmat - m_t)
            w_inter = jnp.exp(inter - m_t)
            num = (jnp.dot(vt, sw.astype(BF16), preferred_element_type=F32)
                   + w_inter * lax.dot_general(c_prev.astype(BF16), qh, _NT, preferred_element_type=F32))
            qn = lax.dot_general(n_prev.astype(BF16), qh, _NT, preferred_element_type=F32)[0:1, :]
            den = jnp.sum(sw, axis=0, keepdims=True) + w_inter * qn
            ht = num * (1.0 / jnp.maximum(jnp.abs(den), jnp.exp(-m_t)))

            mu = jnp.mean(ht, axis=0, keepdims=True)
            dc = ht - mu
            var = jnp.mean(dc * dc, axis=0, keepdims=True)
            yt = dc * lax.rsqrt(var + EPS) * gh_ref[hs, :] * _sigmoid(ot_ref[hs, rows])
            out_ref[rows, hs] = yt.T.astype(out_ref.dtype)

            g_r = b_last - b_r + i_r
            m_new = jnp.maximum(b_last + m_prev, jnp.max(g_r, axis=-1, keepdims=True))
            w_c = jnp.exp(a_c + (b_last - m_new))
            decay = jnp.exp(b_last + m_prev - m_new)
            kw = kh.astype(F32) * w_c
            c_ref[h] = decay * c_prev + jnp.dot(vt, kw.astype(BF16), preferred_element_type=F32)
            n_ref[h] = decay * n_prev + jnp.sum(kw, axis=0, keepdims=True)
            m_ref[h] = jnp.broadcast_to(m_new, (SUBLANES, LANES))


def _mlstm(mq, mk, mvt, mot, gcol, grow, g_mhead_rep):
    B, S, W = mq.shape
    tb = ML_BLOCK
    tok = lambda w: pl.BlockSpec((None, tb, w), lambda b, j: (b, j, 0))
    feat = lambda w: pl.BlockSpec((None, w, tb), lambda b, j: (b, 0, j))
    consts = _mlstm_consts() + (g_mhead_rep,)
    return pl.pallas_call(
        _mlstm_kernel,
        grid=(B, S // tb),
        in_specs=[tok(W), tok(W), feat(W), feat(W), tok(LANES), feat(SUBLANES)]
        + [_const_spec(w.shape) for w in consts],
        out_specs=tok(W),
        out_shape=jax.ShapeDtypeStruct((B, S, W), BF16),
        scratch_shapes=[pltpu.VMEM((M_HEADS, M_HEAD_DIM, M_HEAD_DIM), F32),
                        pltpu.VMEM((M_HEADS, SUBLANES, M_HEAD_DIM), F32),
                        pltpu.VMEM((M_HEADS, SUBLANES, LANES), F32)],
        compiler_params=pltpu.CompilerParams(
            dimension_semantics=("arbitrary", "arbitrary"), vmem_limit_bytes=VMEM_LIMIT),
        name="mlstm",
    )(mq, mk, mvt, mot, gcol, grow, *consts)


def _tile_delta(qi, ki):
    return qi - ki + (NKB - 1) * TQ // LANES


def _partial_deltas():
    T = LANES
    deltas = sorted({_tile_delta(qi, ki) for qi in range(TQ // T) for ki in range(NKB * TQ // T)})
    return [d for d in deltas if T * d - (T - 1) < MAX_REL and T * d + (T - 1) > -MAX_REL]


def _bias_rows(rel_bias):
    T = LANES
    x = np.arange(2 * T)
    xs = np.where(x < T, x, x - 2 * T)
    rows = []
    for d in _partial_deltas():
        idx = np.clip(T * d - xs, -MAX_REL, MAX_REL) + MAX_REL
        rows.append(rel_bias[:, idx])
    return jnp.stack(rows, axis=1).astype(F32)


def _build_band_bias(rb_ref, rows_ref, bias_ref):
    T = LANES
    rr = lax.broadcasted_iota(jnp.int32, (T, T), 0)
    cc = lax.broadcasted_iota(jnp.int32, (T, T), 1)
    rr2 = lax.broadcasted_iota(jnp.int32, (T, 2 * T), 0)
    back = (NKB - 1) * TQ
    partial = _partial_deltas()
    for h in range(A_HEADS):
        toeplitz = {}
        for n, delta in enumerate(partial):
            tab = jnp.broadcast_to(rows_ref[h, n:n + 1, :], (T, 2 * T))
            for bit in range(T.bit_length() - 1):
                rolled = pltpu.roll(tab, 1 << bit, 1)
                tab = jnp.where(((rr2 >> bit) & 1) == 1, rolled, tab)
            toeplitz[delta] = tab[:, :T]

        for qi in range(TQ // T):
            for ki in range(NKB * TQ // T):
                qc = (rr + qi * T + back) // CHUNK
                kc = (cc + ki * T) // CHUNK
                visible = (kc <= qc) & (kc >= qc - BAND_CHUNKS)
                q_lo, q_hi = (qi * T + back) // CHUNK, (qi * T + T - 1 + back) // CHUNK
                k_lo, k_hi = (ki * T) // CHUNK, (ki * T + T - 1) // CHUNK
                delta = _tile_delta(qi, ki)
                if k_lo > q_hi or k_hi < q_lo - BAND_CHUNKS:
                    tile = jnp.full((T, T), NEG, F32)
                else:
                    if delta in toeplitz:
                        vals = toeplitz[delta]
                    else:
                        far = 2 * MAX_REL if delta > 0 else 0
                        vals = jnp.full((T, T), rb_ref[h, far], F32)
                    tile = jnp.where(visible, vals * LOG2E, NEG)
                bias_ref[h, qi * T:(qi + 1) * T, ki * T:(ki + 1) * T] = tile


def _attn_kernel(rb_ref, rows_ref, q_ref, *refs):
    k_refs = refs[:NKB]
    v_refs = refs[NKB:2 * NKB]
    out_ref = refs[2 * NKB]
    bias_ref = refs[2 * NKB + 1]
    j = pl.program_id(1)
    nk = NKB * TQ

    @pl.when((pl.program_id(0) == 0) & (j == 0))
    def _():
        _build_band_bias(rb_ref, rows_ref, bias_ref)

    def heads(at_start):
        if at_start:
            kpos = lax.broadcasted_iota(jnp.int32, (1, nk), 1)
            valid = kpos >= (NKB - 1 - j) * TQ
        for h in range(A_HEADS):
            hs = slice(h * A_HEAD_DIM, (h + 1) * A_HEAD_DIM)
            qh = q_ref[:, hs]
            kh = jnp.concatenate([r[:, hs] for r in k_refs], axis=0)
            vh = jnp.concatenate([r[:, hs] for r in v_refs], axis=0)
            s = lax.dot_general(qh, kh, _NT, preferred_element_type=F32) + bias_ref[h]
            if at_start:
                s = jnp.where(valid, s, NEG)
            m = jnp.max(s, axis=-1, keepdims=True)
            p = jnp.exp2(s - m)
            l = jnp.sum(p, axis=-1, keepdims=True)
            o = jnp.dot(p.astype(BF16), vh, preferred_element_type=F32) / l
            out_ref[:, hs] = o.astype(out_ref.dtype)

    pl.when(j < NKB - 1)(functools.partial(heads, True))
    pl.when(j >= NKB - 1)(functools.partial(heads, False))


def _attention(aq, ak, av, rel_bias):
    B, S, W = aq.shape
    blk = lambda back: pl.BlockSpec((None, TQ, W), lambda b, j: (b, jnp.maximum(j - back, 0), 0))
    kv_specs = [blk(NKB - 1 - n) for n in range(NKB)]
    rows = _bias_rows(rel_bias)
    return pl.pallas_call(
        _attn_kernel,
        grid=(B, S // TQ),
        in_specs=[pl.BlockSpec(memory_space=pltpu.SMEM), _const_spec(rows.shape), blk(0)] + kv_specs + kv_specs,
        out_specs=blk(0),
        out_shape=jax.ShapeDtypeStruct((B, S, W), BF16),
        scratch_shapes=[pltpu.VMEM((A_HEADS, TQ, NKB * TQ), F32)],
        compiler_params=pltpu.CompilerParams(
            dimension_semantics=("arbitrary", "arbitrary"), vmem_limit_bytes=VMEM_LIMIT),
        name="band_attn",
    )(rel_bias.astype(F32), rows, aq, *([ak] * NKB), *([av] * NKB))


def _gelu_tanh(x):
    return 0.5 * x * (1.0 + jnp.tanh(0.7978845608028654 * (x + 0.044715 * (x * x * x))))


def _mixer_kernel(x_ref, mh_ref, ah_ref, p_ref, woa_ref, wob_ref, gffn_ref, wug_ref, wuv_ref,
                  cwg_ref, cwv_ref, cbg_ref, cbv_ref, wd_ref, gple_ref, wpg_ref, wpp_ref, gfin_ref,
                  out_ref, a_ref, acc_ref, u0_ref, u1_ref, carry_ref):
    tm = x_ref.shape[0]

    @pl.when(pl.program_id(1) == 0)
    def _():
        carry_ref[...] = jnp.zeros(carry_ref.shape, F32)

    h1 = (x_ref[...]
          + jnp.dot(mh_ref[...], woa_ref[...], preferred_element_type=F32)
          + jnp.dot(ah_ref[...], wob_ref[...], preferred_element_type=F32))
    acc_ref[...] = h1
    a_ref[...] = _rms(h1, gffn_ref[...]).astype(BF16)

    def up(c, u_ref):
        a = a_ref[...]
        u_ref[0, 0:SUBLANES, :] = carry_ref[c, 0]
        u_ref[1, 0:SUBLANES, :] = carry_ref[c, 1]
        u_ref[0, SUBLANES:, :] = jnp.dot(a, wug_ref[c], preferred_element_type=F32)
        u_ref[1, SUBLANES:, :] = jnp.dot(a, wuv_ref[c], preferred_element_type=F32)

    def conv(u_ref, part, w, b):
        y = b + w[FFN_CONV - 1:FFN_CONV, :] * u_ref[part, SUBLANES:, :]
        for k in range(1, FFN_CONV):
            y = y + w[FFN_CONV - 1 - k:FFN_CONV - k, :] * u_ref[part, SUBLANES - k:SUBLANES - k + tm, :]
        return y

    def down(c, u_ref):
        gate = conv(u_ref, 0, cwg_ref[c], cbg_ref[c])
        val = conv(u_ref, 1, cwv_ref[c], cbv_ref[c])
        carry_ref[c, 0] = u_ref[0, tm:tm + SUBLANES, :]
        carry_ref[c, 1] = u_ref[1, tm:tm + SUBLANES, :]
        act = (_gelu_tanh(gate) * val).astype(BF16)
        acc_ref[...] += jnp.dot(act, wd_ref[c], preferred_element_type=F32)

    assert NCH % 2 == 1
    up(0, u0_ref)

    def body(i, carry):
        c = 2 * i
        up(c + 1, u1_ref)
        down(c, u0_ref)
        up(c + 2, u0_ref)
        down(c + 1, u1_ref)
        return carry

    lax.fori_loop(0, NCH // 2, body, 0)
    down(NCH - 1, u0_ref)

    h2 = acc_ref[...]
    gate = _sigmoid(jnp.dot(_rms(h2, gple_ref[...]).astype(BF16), wpg_ref[...], preferred_element_type=F32))
    emb = jnp.dot(p_ref[...].astype(BF16), wpp_ref[...], preferred_element_type=F32)
    h3 = h2 + emb * gate
    out_ref[...] = _rms(h3, gfin_ref[...])


def _mixer(x, mh, ah, p, weights):
    B, S, D = x.shape
    tm = TM_FF
    tok = lambda w: pl.BlockSpec((None, tm, w), lambda b, i: (b, i, 0))
    return pl.pallas_call(
        _mixer_kernel,
        grid=(B, S // tm),
        in_specs=[tok(D), tok(M_WIDTH), tok(A_WIDTH), tok(D_PLE)] + [_const_spec(w.shape) for w in weights],
        out_specs=tok(D),
        out_shape=jax.ShapeDtypeStruct((B, S, D), F32),
        scratch_shapes=[pltpu.VMEM((tm, D), BF16),
                        pltpu.VMEM((tm, D), F32),
                        pltpu.VMEM((2, tm + SUBLANES, FC), F32),
                        pltpu.VMEM((2, tm + SUBLANES, FC), F32),
                        pltpu.VMEM((NCH, 2, SUBLANES, FC), F32)],
        compiler_params=pltpu.CompilerParams(
            dimension_semantics=("arbitrary", "arbitrary"), vmem_limit_bytes=VMEM_LIMIT),
        name="mixer",
    )(x, mh, ah, p, *weights)


def _layer(h, p, g_mix, w_in, b_igate, b_fgate, w_qk_conv, b_qk_conv, g_mhead, rel_bias, w_out,
           g_ffn, w_ffn_up, w_ffn_conv, b_ffn_conv, w_ffn_down, g_ple, w_ple_gate, w_ple_proj, g_out):
    B, S, D = h.shape
    row = lambda v: v.reshape(1, -1).astype(F32)
    m_v = 2 * M_WIDTH
    m_i = 4 * M_WIDTH
    a_q = m_i + 2 * M_HEADS
    wqk = w_in[:, :m_v].astype(BF16)
    wvot = w_in[:, m_v:m_i].T.astype(BF16)
    wgate = w_in[:, m_i:a_q]
    wg = jnp.pad(wgate, ((0, 0), (0, LANES - 2 * M_HEADS))).astype(BF16)
    wgt = wgate.T.astype(BF16)
    watt = w_in[:, a_q:].astype(BF16)
    gbias = jnp.concatenate([b_igate, b_fgate]).astype(F32)
    gb = jnp.pad(gbias, (0, LANES - 2 * M_HEADS)).reshape(1, LANES)
    gbt = gbias.reshape(2 * M_HEADS, 1)

    mq, mk, mvt, mot, gcol, grow, aq, ak, av = _inproj(
        h, row(g_mix), wqk, wvot, wg, wgt, watt, w_qk_conv.astype(F32), row(b_qk_conv), gb, gbt)

    g_mhead_rep = jnp.broadcast_to(g_mhead.astype(F32)[:, None], (M_WIDTH, LANES))
    mh = _mlstm(mq, mk, mvt, mot, gcol, grow, g_mhead_rep)
    ah = _attention(aq, ak, av, rel_bias)

    wup = w_ffn_up.astype(BF16)
    split_cols = lambda w: w.reshape(w.shape[0], NCH, FC).transpose(1, 0, 2)
    weights = (
        w_out[:M_WIDTH].astype(BF16), w_out[M_WIDTH:].astype(BF16), row(g_ffn),
        split_cols(wup[:, :D_FF]), split_cols(wup[:, D_FF:]),
        split_cols(w_ffn_conv[:, :D_FF].astype(F32)), split_cols(w_ffn_conv[:, D_FF:].astype(F32)),
        b_ffn_conv[:D_FF].astype(F32).reshape(NCH, 1, FC), b_ffn_conv[D_FF:].astype(F32).reshape(NCH, 1, FC),
        w_ffn_down.astype(BF16).reshape(NCH, FC, D),
        row(g_ple), w_ple_gate.astype(BF16), w_ple_proj.astype(BF16), g_out,
    )
    return _mixer(h, mh, ah, p, weights)


def kernel(x, p, g_mix, w_in, b_igate, b_fgate, w_qk_conv, b_qk_conv, g_mhead, rel_bias, w_out, g_ffn,
           w_ffn_up, w_ffn_conv, b_ffn_conv, w_ffn_down, g_ple, w_ple_gate, w_ple_proj, g_final):
    depth = w_in.shape[0]
    assert depth == 1, "the final norm is fused into the last layer's channel-mixer kernel"
    i = 0
    return _layer(x, p[i], g_mix[i], w_in[i], b_igate[i], b_fgate[i], w_qk_conv[i], b_qk_conv[i],
                  g_mhead[i], rel_bias[i], w_out[i], g_ffn[i], w_ffn_up[i], w_ffn_conv[i], b_ffn_conv[i],
                  w_ffn_down[i], g_ple[i], w_ple_gate[i], w_ple_proj[i], g_final.reshape(1, -1).astype(F32))
```

```python
import functools

import numpy as np
import jax
import jax.numpy as jnp
from jax import lax
from jax.experimental import pallas as pl
from jax.experimental.pallas import tpu as pltpu

F32 = jnp.float32
BF16 = jnp.bfloat16

D_MODEL = 1024
CHUNK = 64
M_HEADS = 4
M_WIDTH = 512
M_HEAD_DIM = 128
QK_CONV = 4
A_HEADS = 8
A_WIDTH = 512
A_HEAD_DIM = 64
BAND_CHUNKS = 8
MAX_REL = 128
D_FF = 2816
FFN_CONV = 3
D_PLE = 256
EPS = 1e-6

LANES = 128
SUBLANES = 8
NEG = -1e30
LOG2E = 1.4426950408889634

TM_IN = 512
ML_CHUNK = 128
ML_BLOCK = 512
TQ = 256
NKB = 1 + (BAND_CHUNKS * CHUNK) // TQ
HG = 2 * LANES // A_HEAD_DIM
TM_FF = 512
FC = 256
NCH = D_FF // FC

VMEM_LIMIT = 56 * 1024 * 1024

_NT = (((1,), (1,)), ((), ()))


def _rms(x, g):
    return x * lax.rsqrt(jnp.mean(x * x, axis=-1, keepdims=True) + EPS) * g


def _sigmoid(x):
    return 1.0 / (1.0 + jnp.exp(-x))


def _log_sigmoid(x):
    return jnp.minimum(x, 0.0) - jnp.log1p(jnp.exp(-jnp.abs(x)))


def _split3(x):
    hi = x.astype(BF16)
    r = x - hi.astype(F32)
    mid = r.astype(BF16)
    lo = (r - mid.astype(F32)).astype(BF16)
    return hi, mid, lo


def _const_spec(shape):
    n = len(shape)
    return pl.BlockSpec(shape, lambda *_: (0,) * n, pipeline_mode=pl.Buffered(1))


def _inproj_kernel(x_ref, g_ref, wqk_ref, wvot_ref, wg_ref, wgt_ref, watt_ref, cw_ref, cb_ref,
                   gb_ref, gbt_ref,
                   mq_ref, mk_ref, mvt_ref, mot_ref, gcol_ref, grow_ref, aq_ref, ak_ref, av_ref,
                   zext_ref):
    tm = x_ref.shape[0]

    @pl.when(pl.program_id(1) == 0)
    def _():
        zext_ref[0:SUBLANES, :] = jnp.zeros((SUBLANES, 2 * M_WIDTH), F32)

    a = _rms(x_ref[...], g_ref[...]).astype(BF16)

    zqk = jnp.dot(a, wqk_ref[...], preferred_element_type=F32)
    zext_ref[SUBLANES:, :] = zqk
    cw = cw_ref[...]
    acc = cb_ref[...] + cw[QK_CONV - 1:QK_CONV, :] * zqk
    for k in range(1, QK_CONV):
        acc = acc + cw[QK_CONV - 1 - k:QK_CONV - k, :] * zext_ref[SUBLANES - k:SUBLANES - k + tm, :]
    zext_ref[0:SUBLANES, :] = zext_ref[tm:tm + SUBLANES, :]
    qk = acc * _sigmoid(acc)
    mq_ref[...] = qk[:, :M_WIDTH].astype(BF16)
    mk_ref[...] = (qk[:, M_WIDTH:] * (M_HEAD_DIM ** -0.5)).astype(BF16)

    zvot = lax.dot_general(wvot_ref[...], a, _NT, preferred_element_type=F32)
    mvt_ref[...] = zvot[:M_WIDTH].astype(BF16)
    mot_ref[...] = zvot[M_WIDTH:]

    zg = jnp.dot(a, wg_ref[...], preferred_element_type=F32) + gb_ref[...]
    lane = lax.broadcasted_iota(jnp.int32, zg.shape, 1)
    gcol_ref[...] = jnp.where(lane < M_HEADS, zg, _log_sigmoid(zg))
    zgt = lax.dot_general(wgt_ref[...], a, _NT, preferred_element_type=F32) + gbt_ref[...]
    row = lax.broadcasted_iota(jnp.int32, zgt.shape, 0)
    grow_ref[...] = jnp.where(row < M_HEADS, zgt, _log_sigmoid(zgt))

    zatt = jnp.dot(a, watt_ref[...], preferred_element_type=F32)
    aq_ref[...] = (zatt[:, :A_WIDTH] * (LOG2E * A_HEAD_DIM ** -0.5)).astype(BF16)
    ak_ref[...] = zatt[:, A_WIDTH:2 * A_WIDTH].astype(BF16)
    av_ref[...] = zatt[:, 2 * A_WIDTH:].astype(BF16)


def _inproj(x, g_mix, wqk, wvot, wg, wgt, watt, cw, cb, gb, gbt):
    B, S, D = x.shape
    tm = TM_IN
    tok = lambda w: pl.BlockSpec((None, tm, w), lambda b, i: (b, i, 0))
    feat = lambda w: pl.BlockSpec((None, w, tm), lambda b, i: (b, 0, i))
    out_shape = (
        jax.ShapeDtypeStruct((B, S, M_WIDTH), BF16),
        jax.ShapeDtypeStruct((B, S, M_WIDTH), BF16),
        jax.ShapeDtypeStruct((B, M_WIDTH, S), BF16),
        jax.ShapeDtypeStruct((B, M_WIDTH, S), F32),
        jax.ShapeDtypeStruct((B, S, LANES), F32),
        jax.ShapeDtypeStruct((B, SUBLANES, S), F32),
        jax.ShapeDtypeStruct((B, S, A_WIDTH), BF16),
        jax.ShapeDtypeStruct((B, S, A_WIDTH), BF16),
        jax.ShapeDtypeStruct((B, S, A_WIDTH), BF16),
    )
    out_specs = (tok(M_WIDTH), tok(M_WIDTH), feat(M_WIDTH), feat(M_WIDTH), tok(LANES), feat(SUBLANES),
                 tok(A_WIDTH), tok(A_WIDTH), tok(A_WIDTH))
    consts = (g_mix, wqk, wvot, wg, wgt, watt, cw, cb, gb, gbt)
    return pl.pallas_call(
        _inproj_kernel,
        grid=(B, S // tm),
        in_specs=[tok(D)] + [_const_spec(w.shape) for w in consts],
        out_specs=out_specs,
        out_shape=out_shape,
        scratch_shapes=[pltpu.VMEM((tm + SUBLANES, 2 * M_WIDTH), F32)],
        compiler_params=pltpu.CompilerParams(
            dimension_semantics=("arbitrary", "arbitrary"), vmem_limit_bytes=VMEM_LIMIT),
        name="inproj",
    )(x, *consts)


def _mlstm_consts():
    L = ML_CHUNK
    s = np.arange(L)
    lower = (s[:, None] >= s[None, :]).astype(np.float32)
    upper = lower.T
    sel = np.zeros((LANES, M_HEADS * LANES), np.float32)
    for h in range(M_HEADS):
        sel[h, h * LANES:(h + 1) * LANES] = 1.0
        sel[M_HEADS + h, h * LANES:(h + 1) * LANES] = -1.0
    to_bf16 = lambda m: jnp.asarray(m, BF16)
    return (to_bf16(np.concatenate([lower] * 3, axis=1)),
            to_bf16(np.concatenate([upper] * 3, axis=0)),
            to_bf16(np.concatenate([sel] * 3, axis=0)))


def _mlstm_kernel(q_ref, k_ref, vt_ref, ot_ref, gcol_ref, grow_ref, lower3_ref, upper3_ref, sel3_ref, gh_ref,
                  out_ref, c_ref, n_ref, m_ref):
    L = ML_CHUNK
    dh = M_HEAD_DIM

    @pl.when(pl.program_id(1) == 0)
    def _():
        c_ref[...] = jnp.zeros(c_ref.shape, F32)
        n_ref[...] = jnp.zeros(n_ref.shape, F32)
        m_ref[...] = jnp.zeros(m_ref.shape, F32)

    rr = lax.broadcasted_iota(jnp.int32, (L, L), 0)
    cc = lax.broadcasted_iota(jnp.int32, (L, L), 1)
    causal_t = rr <= cc
    lane = lax.broadcasted_iota(jnp.int32, (L, LANES), 1)
    PR = 2 * SUBLANES

    def blockdiag(x, y):
        zx = jnp.zeros((x.shape[0], y.shape[1]), x.dtype)
        zy = jnp.zeros((y.shape[0], x.shape[1]), y.dtype)
        return jnp.concatenate([jnp.concatenate([x, zx], axis=1), jnp.concatenate([zy, y], axis=1)], axis=0)

    for c in range(q_ref.shape[0] // L):
        rows = slice(c * L, (c + 1) * L)
        gc = gcol_ref[rows, :]
        gr = grow_ref[:, rows]
        bc = jnp.dot(lower3_ref[...], jnp.concatenate(_split3(gc), axis=0), preferred_element_type=F32)
        br = jnp.dot(jnp.concatenate(_split3(gr), axis=1), upper3_ref[...], preferred_element_type=F32)
        ib = jnp.where(lane < M_HEADS, gc, bc)
        a_all = jnp.dot(jnp.concatenate(_split3(ib), axis=1), sel3_ref[...], preferred_element_type=F32)
        for g in range(M_HEADS // 2):
            heads = (2 * g, 2 * g + 1)
            ps = slice(2 * g * dh, (2 * g + 2) * dh)
            qp = q_ref[rows, ps]
            kp = k_ref[rows, ps]
            vtp = vt_ref[ps, rows]
            c_prev = [c_ref[h] for h in heads]
            n_prev = [n_ref[h] for h in heads]
            m_prev = [m_ref[h][0:1, 0:1] for h in heads]

            st = lax.dot_general(kp, blockdiag(qp[:, :dh], qp[:, dh:]), _NT, preferred_element_type=F32)
            n2 = [jnp.concatenate([n, n], axis=0).astype(BF16) for n in n_prev]
            cq = lax.dot_general(
                jnp.concatenate([blockdiag(c_prev[0].astype(BF16), c_prev[1].astype(BF16)),
                                 blockdiag(n2[0], n2[1])], axis=0),
                qp, _NT, preferred_element_type=F32)

            rhs, keep = [], []
            for e, h in enumerate(heads):
                a_c = a_all[:, h * dh:(h + 1) * dh]
                i_r = gr[h:h + 1, :]
                b_r = br[M_HEADS + h:M_HEADS + h + 1, :]
                b_last = b_r[:, L - 1:L]
                dmat = jnp.where(causal_t, a_c + b_r, -jnp.inf)
                inter = b_r + m_prev[e]
                m_t = jnp.maximum(inter, jnp.max(dmat, axis=0, keepdims=True))
                sw = st[:, e * L:(e + 1) * L] * jnp.exp(dmat - m_t)
                g_r = b_last - b_r + i_r
                m_new = jnp.maximum(b_last + m_prev[e], jnp.max(g_r, axis=-1, keepdims=True))
                w_c = jnp.exp(a_c + (b_last - m_new))
                kw = kp[:, e * dh:(e + 1) * dh].astype(F32) * w_c
                rhs.append(jnp.concatenate([sw.astype(BF16), kw.astype(BF16)], axis=1))
                keep.append((m_t, jnp.exp(inter - m_t), jnp.exp(b_last + m_prev[e] - m_new), m_new))

            ones = jnp.ones((PR, L), BF16)
            vs = jnp.dot(
                jnp.concatenate([blockdiag(vtp[:dh], vtp[dh:]), blockdiag(ones, ones)], axis=0),
                jnp.concatenate(rhs, axis=0), preferred_element_type=F32)

            for e, h in enumerate(heads):
                hs = slice(h * dh, (h + 1) * dh)
                m_t, w_inter, decay, m_new = keep[e]
                vr = slice(e * dh, (e + 1) * dh)
                nr = 2 * dh + e * PR
                num = vs[vr, :L] + w_inter * cq[vr, :]
                den = vs[nr:nr + 1, :L] + w_inter * cq[nr:nr + 1, :]
                ht = num * (1.0 / jnp.maximum(jnp.abs(den), jnp.exp(-m_t)))

                mu = jnp.mean(ht, axis=0, keepdims=True)
                dc = ht - mu
                var = jnp.mean(dc * dc, axis=0, keepdims=True)
                yt = dc * lax.rsqrt(var + EPS) * gh_ref[hs, :] * _sigmoid(ot_ref[hs, rows])
                out_ref[rows, hs] = yt.T.astype(out_ref.dtype)

                c_ref[h] = decay * c_prev[e] + vs[vr, L:]
                n_ref[h] = decay * n_prev[e] + vs[nr:nr + SUBLANES, L:]
                m_ref[h] = jnp.broadcast_to(m_new, (SUBLANES, LANES))


def _mlstm(mq, mk, mvt, mot, gcol, grow, g_mhead_rep):
    B, S, W = mq.shape
    tb = ML_BLOCK
    tok = lambda w: pl.BlockSpec((None, tb, w), lambda b, j: (b, j, 0))
    feat = lambda w: pl.BlockSpec((None, w, tb), lambda b, j: (b, 0, j))
    consts = _mlstm_consts() + (g_mhead_rep,)
    return pl.pallas_call(
        _mlstm_kernel,
        grid=(B, S // tb),
        in_specs=[tok(W), tok(W), feat(W), feat(W), tok(LANES), feat(SUBLANES)]
        + [_const_spec(w.shape) for w in consts],
        out_specs=tok(W),
        out_shape=jax.ShapeDtypeStruct((B, S, W), BF16),
        scratch_shapes=[pltpu.VMEM((M_HEADS, M_HEAD_DIM, M_HEAD_DIM), F32),
                        pltpu.VMEM((M_HEADS, SUBLANES, M_HEAD_DIM), F32),
                        pltpu.VMEM((M_HEADS, SUBLANES, LANES), F32)],
        compiler_params=pltpu.CompilerParams(
            dimension_semantics=("arbitrary", "arbitrary"), vmem_limit_bytes=VMEM_LIMIT),
        name="mlstm",
    )(mq, mk, mvt, mot, gcol, grow, *consts)


def _tile_delta(qi, ki):
    return qi - ki + (NKB - 1) * TQ // LANES


def _partial_deltas():
    T = LANES
    deltas = sorted({_tile_delta(qi, ki) for qi in range(TQ // T) for ki in range(NKB * TQ // T)})
    return [d for d in deltas if T * d - (T - 1) < MAX_REL and T * d + (T - 1) > -MAX_REL]


def _bias_rows(rel_bias):
    T = LANES
    x = np.arange(2 * T)
    xs = np.where(x < T, x, x - 2 * T)
    rows = []
    for d in _partial_deltas():
        idx = np.clip(T * d - xs, -MAX_REL, MAX_REL) + MAX_REL
        rows.append(rel_bias[:, idx])
    return jnp.stack(rows, axis=1).astype(F32)


def _build_band_bias(rb_ref, rows_ref, bias_ref):
    T = LANES
    rr = lax.broadcasted_iota(jnp.int32, (T, T), 0)
    cc = lax.broadcasted_iota(jnp.int32, (T, T), 1)
    rr2 = lax.broadcasted_iota(jnp.int32, (T, 2 * T), 0)
    back = (NKB - 1) * TQ
    partial = _partial_deltas()
    for h in range(A_HEADS):
        toeplitz = {}
        for n, delta in enumerate(partial):
            tab = jnp.broadcast_to(rows_ref[h, n:n + 1, :], (T, 2 * T))
            for bit in range(T.bit_length() - 1):
                rolled = pltpu.roll(tab, 1 << bit, 1)
                tab = jnp.where(((rr2 >> bit) & 1) == 1, rolled, tab)
            toeplitz[delta] = tab[:, :T]

        for qi in range(TQ // T):
            for ki in range(NKB * TQ // T):
                qc = (rr + qi * T + back) // CHUNK
                kc = (cc + ki * T) // CHUNK
                visible = (kc <= qc) & (kc >= qc - BAND_CHUNKS)
                q_lo, q_hi = (qi * T + back) // CHUNK, (qi * T + T - 1 + back) // CHUNK
                k_lo, k_hi = (ki * T) // CHUNK, (ki * T + T - 1) // CHUNK
                delta = _tile_delta(qi, ki)
                if k_lo > q_hi or k_hi < q_lo - BAND_CHUNKS:
                    tile = jnp.full((T, T), NEG, F32)
                else:
                    if delta in toeplitz:
                        vals = toeplitz[delta]
                    else:
                        far = 2 * MAX_REL if delta > 0 else 0
                        vals = jnp.full((T, T), rb_ref[h, far], F32)
                    tile = jnp.where(visible, vals * LOG2E, NEG)
                bias_ref[h, qi * T:(qi + 1) * T, ki * T:(ki + 1) * T] = tile


def _attn_kernel(rb_ref, rows_ref, q_ref, *refs):
    k_refs = refs[:NKB]
    v_refs = refs[NKB:2 * NKB]
    out_ref = refs[2 * NKB]
    bias_ref = refs[2 * NKB + 1]
    j = pl.program_id(1)
    nk = NKB * TQ

    @pl.when((pl.program_id(0) == 0) & (j == 0))
    def _():
        _build_band_bias(rb_ref, rows_ref, bias_ref)

    def heads(at_start):
        if at_start:
            kpos = lax.broadcasted_iota(jnp.int32, (1, nk), 1)
            valid = kpos >= (NKB - 1 - j) * TQ
        gw = HG * A_HEAD_DIM
        lane_head = lax.broadcasted_iota(jnp.int32, (1, gw), 1) // A_HEAD_DIM
        onehot = [(lane_head == e).astype(BF16) for e in range(HG)]
        for g in range(A_HEADS // HG):
            gs = slice(g * gw, (g + 1) * gw)
            kq = jnp.concatenate([r[:, gs] for r in k_refs], axis=0)
            vq = jnp.concatenate([r[:, gs] for r in v_refs], axis=0)
            kbd = jnp.concatenate([kq * onehot[e] for e in range(HG)], axis=0)
            vbd = jnp.concatenate([vq * onehot[e] for e in range(HG)], axis=0)
            s = lax.dot_general(q_ref[:, gs], kbd, _NT, preferred_element_type=F32)
            probs, scale = [], None
            for e in range(HG):
                sh = s[:, e * nk:(e + 1) * nk] + bias_ref[g * HG + e]
                if at_start:
                    sh = jnp.where(valid, sh, NEG)
                m = jnp.max(sh, axis=-1, keepdims=True)
                p = jnp.exp2(sh - m)
                rl = 1.0 / jnp.sum(p, axis=-1, keepdims=True)
                probs.append(p.astype(BF16))
                scale = rl if e == 0 else jnp.where(lane_head >= e, rl, scale)
            o = jnp.dot(jnp.concatenate(probs, axis=1), vbd, preferred_element_type=F32) * scale
            out_ref[:, gs] = o.astype(out_ref.dtype)

    pl.when(j < NKB - 1)(functools.partial(heads, True))
    pl.when(j >= NKB - 1)(functools.partial(heads, False))


def _attention(aq, ak, av, rel_bias):
    B, S, W = aq.shape
    blk = lambda back: pl.BlockSpec((None, TQ, W), lambda b, j: (b, jnp.maximum(j - back, 0), 0))
    kv_specs = [blk(NKB - 1 - n) for n in range(NKB)]
    rows = _bias_rows(rel_bias)
    return pl.pallas_call(
        _attn_kernel,
        grid=(B, S // TQ),
        in_specs=[pl.BlockSpec(memory_space=pltpu.SMEM), _const_spec(rows.shape), blk(0)] + kv_specs + kv_specs,
        out_specs=blk(0),
        out_shape=jax.ShapeDtypeStruct((B, S, W), BF16),
        scratch_shapes=[pltpu.VMEM((A_HEADS, TQ, NKB * TQ), F32)],
        compiler_params=pltpu.CompilerParams(
            dimension_semantics=("arbitrary", "arbitrary"), vmem_limit_bytes=VMEM_LIMIT),
        name="band_attn",
    )(rel_bias.astype(F32), rows, aq, *([ak] * NKB), *([av] * NKB))


def _gelu_tanh(x):
    c = 0.7978845608028654
    half = 0.5 * x
    return half * jnp.tanh(x * (c + (c * 0.044715) * (x * x))) + half


def _mixer_kernel(x_ref, mh_ref, ah_ref, p_ref, woa_ref, wob_ref, gffn_ref, wug_ref, wuv_ref,
                  cwg_ref, cwv_ref, cbg_ref, cbv_ref, wd_ref, gple_ref, wpg_ref, wpp_ref, gfin_ref,
                  out_ref, a_ref, acc_ref, u0_ref, u1_ref, carry_ref):
    tm = x_ref.shape[0]

    @pl.when(pl.program_id(1) == 0)
    def _():
        carry_ref[...] = jnp.zeros(carry_ref.shape, F32)

    h1 = (x_ref[...]
          + jnp.dot(mh_ref[...], woa_ref[...], preferred_element_type=F32)
          + jnp.dot(ah_ref[...], wob_ref[...], preferred_element_type=F32))
    acc_ref[...] = h1
    a_ref[...] = _rms(h1, gffn_ref[...]).astype(BF16)

    def up(c, u_ref):
        a = a_ref[...]
        u_ref[0, 0:SUBLANES, :] = carry_ref[c, 0]
        u_ref[1, 0:SUBLANES, :] = carry_ref[c, 1]
        u_ref[0, SUBLANES:, :] = jnp.dot(a, wug_ref[c], preferred_element_type=F32)
        u_ref[1, SUBLANES:, :] = jnp.dot(a, wuv_ref[c], preferred_element_type=F32)

    def conv(u_ref, part, w, b):
        y = b + w[FFN_CONV - 1:FFN_CONV, :] * u_ref[part, SUBLANES:, :]
        for k in range(1, FFN_CONV):
            y = y + w[FFN_CONV - 1 - k:FFN_CONV - k, :] * u_ref[part, SUBLANES - k:SUBLANES - k + tm, :]
        return y

    def down(c, u_ref):
        gate = conv(u_ref, 0, cwg_ref[c], cbg_ref[c])
        val = conv(u_ref, 1, cwv_ref[c], cbv_ref[c])
        carry_ref[c, 0] = u_ref[0, tm:tm + SUBLANES, :]
        carry_ref[c, 1] = u_ref[1, tm:tm + SUBLANES, :]
        act = (_gelu_tanh(gate) * val).astype(BF16)
        acc_ref[...] += jnp.dot(act, wd_ref[c], preferred_element_type=F32)

    assert NCH % 2 == 1
    up(0, u0_ref)

    def body(i, carry):
        c = 2 * i
        up(c + 1, u1_ref)
        down(c, u0_ref)
        up(c + 2, u0_ref)
        down(c + 1, u1_ref)
        return carry

    lax.fori_loop(0, NCH // 2, body, 0)
    down(NCH - 1, u0_ref)

    h2 = acc_ref[...]
    gate = _sigmoid(jnp.dot(_rms(h2, gple_ref[...]).astype(BF16), wpg_ref[...], preferred_element_type=F32))
    emb = jnp.dot(p_ref[...].astype(BF16), wpp_ref[...], preferred_element_type=F32)
    h3 = h2 + emb * gate
    out_ref[...] = _rms(h3, gfin_ref[...])


def _mixer(x, mh, ah, p, weights):
    B, S, D = x.shape
    tm = TM_FF
    tok = lambda w: pl.BlockSpec((None, tm, w), lambda b, i: (b, i, 0))
    return pl.pallas_call(
        _mixer_kernel,
        grid=(B, S // tm),
        in_specs=[tok(D), tok(M_WIDTH), tok(A_WIDTH), tok(D_PLE)] + [_const_spec(w.shape) for w in weights],
        out_specs=tok(D),
        out_shape=jax.ShapeDtypeStruct((B, S, D), F32),
        scratch_shapes=[pltpu.VMEM((tm, D), BF16),
                        pltpu.VMEM((tm, D), F32),
                        pltpu.VMEM((2, tm + SUBLANES, FC), F32),
                        pltpu.VMEM((2, tm + SUBLANES, FC), F32),
                        pltpu.VMEM((NCH, 2, SUBLANES, FC), F32)],
        compiler_params=pltpu.CompilerParams(
            dimension_semantics=("arbitrary", "arbitrary"), vmem_limit_bytes=VMEM_LIMIT),
        name="mixer",
    )(x, mh, ah, p, *weights)


def _layer(h, p, g_mix, w_in, b_igate, b_fgate, w_qk_conv, b_qk_conv, g_mhead, rel_bias, w_out,
           g_ffn, w_ffn_up, w_ffn_conv, b_ffn_conv, w_ffn_down, g_ple, w_ple_gate, w_ple_proj, g_out):
    B, S, D = h.shape
    row = lambda v: v.reshape(1, -1).astype(F32)
    m_v = 2 * M_WIDTH
    m_i = 4 * M_WIDTH
    a_q = m_i + 2 * M_HEADS
    wqk = w_in[:, :m_v].astype(BF16)
    wvot = w_in[:, m_v:m_i].T.astype(BF16)
    wgate = w_in[:, m_i:a_q]
    wg = jnp.pad(wgate, ((0, 0), (0, LANES - 2 * M_HEADS))).astype(BF16)
    wgt = wgate.T.astype(BF16)
    watt = w_in[:, a_q:].astype(BF16)
    gbias = jnp.concatenate([b_igate, b_fgate]).astype(F32)
    gb = jnp.pad(gbias, (0, LANES - 2 * M_HEADS)).reshape(1, LANES)
    gbt = gbias.reshape(2 * M_HEADS, 1)

    mq, mk, mvt, mot, gcol, grow, aq, ak, av = _inproj(
        h, row(g_mix), wqk, wvot, wg, wgt, watt, w_qk_conv.astype(F32), row(b_qk_conv), gb, gbt)

    g_mhead_rep = jnp.broadcast_to(g_mhead.astype(F32)[:, None], (M_WIDTH, LANES))
    mh = _mlstm(mq, mk, mvt, mot, gcol, grow, g_mhead_rep)
    ah = _attention(aq, ak, av, rel_bias)

    wup = w_ffn_up.astype(BF16)
    split_cols = lambda w: w.reshape(w.shape[0], NCH, FC).transpose(1, 0, 2)
    weights = (
        w_out[:M_WIDTH].astype(BF16), w_out[M_WIDTH:].astype(BF16), row(g_ffn),
        split_cols(wup[:, :D_FF]), split_cols(wup[:, D_FF:]),
        split_cols(w_ffn_conv[:, :D_FF].astype(F32)), split_cols(w_ffn_conv[:, D_FF:].astype(F32)),
        b_ffn_conv[:D_FF].astype(F32).reshape(NCH, 1, FC), b_ffn_conv[D_FF:].astype(F32).reshape(NCH, 1, FC),
        w_ffn_down.astype(BF16).reshape(NCH, FC, D),
        row(g_ple), w_ple_gate.astype(BF16), w_ple_proj.astype(BF16), g_out,
    )
    return _mixer(h, mh, ah, p, weights)


def kernel(x, p, g_mix, w_in, b_igate, b_fgate, w_qk_conv, b_qk_conv, g_mhead, rel_bias, w_out, g_ffn,
           w_ffn_up, w_ffn_conv, b_ffn_conv, w_ffn_down, g_ple, w_ple_gate, w_ple_proj, g_final):
    depth = w_in.shape[0]
    assert depth == 1, "the final norm is fused into the last layer's channel-mixer kernel"
    i = 0
    return _layer(x, p[i], g_mix[i], w_in[i], b_igate[i], b_fgate[i], w_qk_conv[i], b_qk_conv[i],
                  g_mhead[i], rel_bias[i], w_out[i], g_ffn[i], w_ffn_up[i], w_ffn_conv[i], b_ffn_conv[i],
                  w_ffn_down[i], g_ple[i], w_ple_gate[i], w_ple_proj[i], g_final.reshape(1, -1).astype(F32))
```

```python
import functools

import numpy as np
import jax
import jax.numpy as jnp
from jax import lax
from jax.experimental import pallas as pl
from jax.experimental.pallas import tpu as pltpu

F32 = jnp.float32
BF16 = jnp.bfloat16

D_MODEL = 1024
CHUNK = 64
M_HEADS = 4
M_WIDTH = 512
M_HEAD_DIM = 128
QK_CONV = 4
A_HEADS = 8
A_WIDTH = 512
A_HEAD_DIM = 64
BAND_CHUNKS = 8
MAX_REL = 128
D_FF = 2816
FFN_CONV = 3
D_PLE = 256
EPS = 1e-6

LANES = 128
SUBLANES = 8
NEG = -1e30
LOG2E = 1.4426950408889634

TM_IN = 512
ML_CHUNK = 128
ML_BLOCK = 512
TQ = 256
NKB = 1 + (BAND_CHUNKS * CHUNK) // TQ
HG = 2 * LANES // A_HEAD_DIM
TM_FF = 512
FC = 256
NCH = D_FF // FC
UNROLL = 2

VMEM_LIMIT = 56 * 1024 * 1024

_NT = (((1,), (1,)), ((), ()))


def _rms(x, g):
    return x * lax.rsqrt(jnp.mean(x * x, axis=-1, keepdims=True) + EPS) * g


def _sigmoid(x):
    return 1.0 / (1.0 + jnp.exp(-x))


def _log_sigmoid(x):
    return jnp.minimum(x, 0.0) - jnp.log1p(jnp.exp(-jnp.abs(x)))


def _split3(x):
    hi = x.astype(BF16)
    r = x - hi.astype(F32)
    mid = r.astype(BF16)
    lo = (r - mid.astype(F32)).astype(BF16)
    return hi, mid, lo


def _const_spec(shape):
    n = len(shape)
    return pl.BlockSpec(shape, lambda *_: (0,) * n, pipeline_mode=pl.Buffered(1))


def _inproj_kernel(x_ref, g_ref, wqk_ref, wvot_ref, wg_ref, wgt_ref, watt_ref, cw_ref, cb_ref,
                   gb_ref, gbt_ref,
                   mq_ref, mk_ref, mvt_ref, mot_ref, gcol_ref, grow_ref, aq_ref, ak_ref, av_ref,
                   zext_ref):
    tm = x_ref.shape[0]

    @pl.when(pl.program_id(1) == 0)
    def _():
        zext_ref[0:SUBLANES, :] = jnp.zeros((SUBLANES, 2 * M_WIDTH), F32)

    a = _rms(x_ref[...], g_ref[...]).astype(BF16)

    zqk = jnp.dot(a, wqk_ref[...], preferred_element_type=F32)
    zext_ref[SUBLANES:, :] = zqk
    cw = cw_ref[...]
    acc = cb_ref[...] + cw[QK_CONV - 1:QK_CONV, :] * zqk
    for k in range(1, QK_CONV):
        acc = acc + cw[QK_CONV - 1 - k:QK_CONV - k, :] * zext_ref[SUBLANES - k:SUBLANES - k + tm, :]
    zext_ref[0:SUBLANES, :] = zext_ref[tm:tm + SUBLANES, :]
    qk = acc * _sigmoid(acc)
    mq_ref[...] = qk[:, :M_WIDTH].astype(BF16)
    mk_ref[...] = (qk[:, M_WIDTH:] * (M_HEAD_DIM ** -0.5)).astype(BF16)

    zvot = lax.dot_general(wvot_ref[...], a, _NT, preferred_element_type=F32)
    mvt_ref[...] = zvot[:M_WIDTH].astype(BF16)
    mot_ref[...] = zvot[M_WIDTH:]

    zg = jnp.dot(a, wg_ref[...], preferred_element_type=F32) + gb_ref[...]
    lane = lax.broadcasted_iota(jnp.int32, zg.shape, 1)
    gcol_ref[...] = jnp.where(lane < M_HEADS, zg, _log_sigmoid(zg))
    zgt = lax.dot_general(wgt_ref[...], a, _NT, preferred_element_type=F32) + gbt_ref[...]
    row = lax.broadcasted_iota(jnp.int32, zgt.shape, 0)
    grow_ref[...] = jnp.where(row < M_HEADS, zgt, _log_sigmoid(zgt))

    zatt = jnp.dot(a, watt_ref[...], preferred_element_type=F32)
    aq_ref[...] = (zatt[:, :A_WIDTH] * (LOG2E * A_HEAD_DIM ** -0.5)).astype(BF16)
    ak_ref[...] = zatt[:, A_WIDTH:2 * A_WIDTH].astype(BF16)
    av_ref[...] = zatt[:, 2 * A_WIDTH:].astype(BF16)


def _inproj(x, g_mix, wqk, wvot, wg, wgt, watt, cw, cb, gb, gbt):
    B, S, D = x.shape
    tm = TM_IN
    tok = lambda w: pl.BlockSpec((None, tm, w), lambda b, i: (b, i, 0))
    feat = lambda w: pl.BlockSpec((None, w, tm), lambda b, i: (b, 0, i))
    out_shape = (
        jax.ShapeDtypeStruct((B, S, M_WIDTH), BF16),
        jax.ShapeDtypeStruct((B, S, M_WIDTH), BF16),
        jax.ShapeDtypeStruct((B, M_WIDTH, S), BF16),
        jax.ShapeDtypeStruct((B, M_WIDTH, S), F32),
        jax.ShapeDtypeStruct((B, S, LANES), F32),
        jax.ShapeDtypeStruct((B, SUBLANES, S), F32),
        jax.ShapeDtypeStruct((B, S, A_WIDTH), BF16),
        jax.ShapeDtypeStruct((B, S, A_WIDTH), BF16),
        jax.ShapeDtypeStruct((B, S, A_WIDTH), BF16),
    )
    out_specs = (tok(M_WIDTH), tok(M_WIDTH), feat(M_WIDTH), feat(M_WIDTH), tok(LANES), feat(SUBLANES),
                 tok(A_WIDTH), tok(A_WIDTH), tok(A_WIDTH))
    consts = (g_mix, wqk, wvot, wg, wgt, watt, cw, cb, gb, gbt)
    return pl.pallas_call(
        _inproj_kernel,
        grid=(B, S // tm),
        in_specs=[tok(D)] + [_const_spec(w.shape) for w in consts],
        out_specs=out_specs,
        out_shape=out_shape,
        scratch_shapes=[pltpu.VMEM((tm + SUBLANES, 2 * M_WIDTH), F32)],
        compiler_params=pltpu.CompilerParams(
            dimension_semantics=("arbitrary", "arbitrary"), vmem_limit_bytes=VMEM_LIMIT),
        name="inproj",
    )(x, *consts)


def _mlstm_consts():
    L = ML_CHUNK
    s = np.arange(L)
    lower = (s[:, None] >= s[None, :]).astype(np.float32)
    upper = lower.T
    sel = np.zeros((LANES, M_HEADS * LANES), np.float32)
    for h in range(M_HEADS):
        sel[h, h * LANES:(h + 1) * LANES] = 1.0
        sel[M_HEADS + h, h * LANES:(h + 1) * LANES] = -1.0
    to_bf16 = lambda m: jnp.asarray(m, BF16)
    return (to_bf16(np.concatenate([lower] * 3, axis=1)),
            to_bf16(np.concatenate([upper] * 3, axis=0)),
            to_bf16(np.concatenate([sel] * 3, axis=0)))


def _mlstm_kernel(q_ref, k_ref, vt_ref, ot_ref, gcol_ref, grow_ref, lower3_ref, upper3_ref, sel3_ref, gh_ref,
                  out_ref, c_ref, n_ref, m_ref):
    L = ML_CHUNK
    dh = M_HEAD_DIM

    @pl.when(pl.program_id(1) == 0)
    def _():
        c_ref[...] = jnp.zeros(c_ref.shape, F32)
        n_ref[...] = jnp.zeros(n_ref.shape, F32)
        m_ref[...] = jnp.zeros(m_ref.shape, F32)

    rr = lax.broadcasted_iota(jnp.int32, (L, L), 0)
    cc = lax.broadcasted_iota(jnp.int32, (L, L), 1)
    causal_t = rr <= cc
    lane = lax.broadcasted_iota(jnp.int32, (L, LANES), 1)
    PR = 2 * SUBLANES

    def blockdiag(x, y):
        zx = jnp.zeros((x.shape[0], y.shape[1]), x.dtype)
        zy = jnp.zeros((y.shape[0], x.shape[1]), y.dtype)
        return jnp.concatenate([jnp.concatenate([x, zx], axis=1), jnp.concatenate([zy, y], axis=1)], axis=0)

    for c in range(q_ref.shape[0] // L):
        rows = slice(c * L, (c + 1) * L)
        gc = gcol_ref[rows, :]
        gr = grow_ref[:, rows]
        bc = jnp.dot(lower3_ref[...], jnp.concatenate(_split3(gc), axis=0), preferred_element_type=F32)
        br = jnp.dot(jnp.concatenate(_split3(gr), axis=1), upper3_ref[...], preferred_element_type=F32)
        ib = jnp.where(lane < M_HEADS, gc, bc)
        a_all = jnp.dot(jnp.concatenate(_split3(ib), axis=1), sel3_ref[...], preferred_element_type=F32)
        for g in range(M_HEADS // 2):
            heads = (2 * g, 2 * g + 1)
            ps = slice(2 * g * dh, (2 * g + 2) * dh)
            qp = q_ref[rows, ps]
            kp = k_ref[rows, ps]
            vtp = vt_ref[ps, rows]
            c_prev = [c_ref[h] for h in heads]
            n_prev = [n_ref[h] for h in heads]
            m_prev = [m_ref[h][0:1, 0:1] for h in heads]

            st = lax.dot_general(kp, blockdiag(qp[:, :dh], qp[:, dh:]), _NT, preferred_element_type=F32)
            n2 = [jnp.concatenate([n, n], axis=0).astype(BF16) for n in n_prev]
            cq = lax.dot_general(
                jnp.concatenate([blockdiag(c_prev[0].astype(BF16), c_prev[1].astype(BF16)),
                                 blockdiag(n2[0], n2[1])], axis=0),
                qp, _NT, preferred_element_type=F32)

            rhs, keep = [], []
            for e, h in enumerate(heads):
                a_c = a_all[:, h * dh:(h + 1) * dh]
                i_r = gr[h:h + 1, :]
                b_r = br[M_HEADS + h:M_HEADS + h + 1, :]
                b_last = b_r[:, L - 1:L]
                dmat = jnp.where(causal_t, a_c + b_r, -jnp.inf)
                inter = b_r + m_prev[e]
                m_t = jnp.maximum(inter, jnp.max(dmat, axis=0, keepdims=True))
                sw = st[:, e * L:(e + 1) * L] * jnp.exp(dmat - m_t)
                g_r = b_last - b_r + i_r
                m_new = jnp.maximum(b_last + m_prev[e], jnp.max(g_r, axis=-1, keepdims=True))
                w_c = jnp.exp(a_c + (b_last - m_new))
                kw = kp[:, e * dh:(e + 1) * dh].astype(F32) * w_c
                rhs.append(jnp.concatenate([sw.astype(BF16), kw.astype(BF16)], axis=1))
                keep.append((m_t, jnp.exp(inter - m_t), jnp.exp(b_last + m_prev[e] - m_new), m_new))

            ones = jnp.ones((PR, L), BF16)
            vs = jnp.dot(
                jnp.concatenate([blockdiag(vtp[:dh], vtp[dh:]), blockdiag(ones, ones)], axis=0),
                jnp.concatenate(rhs, axis=0), preferred_element_type=F32)

            for e, h in enumerate(heads):
                hs = slice(h * dh, (h + 1) * dh)
                m_t, w_inter, decay, m_new = keep[e]
                vr = slice(e * dh, (e + 1) * dh)
                nr = 2 * dh + e * PR
                num = vs[vr, :L] + w_inter * cq[vr, :]
                den = vs[nr:nr + 1, :L] + w_inter * cq[nr:nr + 1, :]
                ht = num * (1.0 / jnp.maximum(jnp.abs(den), jnp.exp(-m_t)))

                mu = jnp.mean(ht, axis=0, keepdims=True)
                dc = ht - mu
                var = jnp.mean(dc * dc, axis=0, keepdims=True)
                yt = dc * lax.rsqrt(var + EPS) * gh_ref[hs, :] * _sigmoid(ot_ref[hs, rows])
                out_ref[rows, hs] = yt.T.astype(out_ref.dtype)

                c_ref[h] = decay * c_prev[e] + vs[vr, L:]
                n_ref[h] = decay * n_prev[e] + vs[nr:nr + SUBLANES, L:]
                m_ref[h] = jnp.broadcast_to(m_new, (SUBLANES, LANES))


def _mlstm(mq, mk, mvt, mot, gcol, grow, g_mhead_rep):
    B, S, W = mq.shape
    tb = ML_BLOCK
    tok = lambda w: pl.BlockSpec((None, tb, w), lambda b, j: (b, j, 0))
    feat = lambda w: pl.BlockSpec((None, w, tb), lambda b, j: (b, 0, j))
    consts = _mlstm_consts() + (g_mhead_rep,)
    return pl.pallas_call(
        _mlstm_kernel,
        grid=(B, S // tb),
        in_specs=[tok(W), tok(W), feat(W), feat(W), tok(LANES), feat(SUBLANES)]
        + [_const_spec(w.shape) for w in consts],
        out_specs=tok(W),
        out_shape=jax.ShapeDtypeStruct((B, S, W), BF16),
        scratch_shapes=[pltpu.VMEM((M_HEADS, M_HEAD_DIM, M_HEAD_DIM), F32),
                        pltpu.VMEM((M_HEADS, SUBLANES, M_HEAD_DIM), F32),
                        pltpu.VMEM((M_HEADS, SUBLANES, LANES), F32)],
        compiler_params=pltpu.CompilerParams(
            dimension_semantics=("arbitrary", "arbitrary"), vmem_limit_bytes=VMEM_LIMIT),
        name="mlstm",
    )(mq, mk, mvt, mot, gcol, grow, *consts)


def _tile_delta(qi, ki):
    return qi - ki + (NKB - 1) * TQ // LANES


def _partial_deltas():
    T = LANES
    deltas = sorted({_tile_delta(qi, ki) for qi in range(TQ // T) for ki in range(NKB * TQ // T)})
    return [d for d in deltas if T * d - (T - 1) < MAX_REL and T * d + (T - 1) > -MAX_REL]


def _bias_rows(rel_bias):
    T = LANES
    x = np.arange(2 * T)
    xs = np.where(x < T, x, x - 2 * T)
    rows = []
    for d in _partial_deltas():
        idx = np.clip(T * d - xs, -MAX_REL, MAX_REL) + MAX_REL
        rows.append(rel_bias[:, idx])
    return jnp.stack(rows, axis=1).astype(F32)


def _build_band_bias(rb_ref, rows_ref, bias_ref):
    T = LANES
    rr = lax.broadcasted_iota(jnp.int32, (T, T), 0)
    cc = lax.broadcasted_iota(jnp.int32, (T, T), 1)
    rr2 = lax.broadcasted_iota(jnp.int32, (T, 2 * T), 0)
    back = (NKB - 1) * TQ
    partial = _partial_deltas()
    for h in range(A_HEADS):
        toeplitz = {}
        for n, delta in enumerate(partial):
            tab = jnp.broadcast_to(rows_ref[h, n:n + 1, :], (T, 2 * T))
            for bit in range(T.bit_length() - 1):
                rolled = pltpu.roll(tab, 1 << bit, 1)
                tab = jnp.where(((rr2 >> bit) & 1) == 1, rolled, tab)
            toeplitz[delta] = tab[:, :T]

        for qi in range(TQ // T):
            for ki in range(NKB * TQ // T):
                qc = (rr + qi * T + back) // CHUNK
                kc = (cc + ki * T) // CHUNK
                visible = (kc <= qc) & (kc >= qc - BAND_CHUNKS)
                q_lo, q_hi = (qi * T + back) // CHUNK, (qi * T + T - 1 + back) // CHUNK
                k_lo, k_hi = (ki * T) // CHUNK, (ki * T + T - 1) // CHUNK
                delta = _tile_delta(qi, ki)
                if k_lo > q_hi or k_hi < q_lo - BAND_CHUNKS:
                    tile = jnp.full((T, T), NEG, F32)
                else:
                    if delta in toeplitz:
                        vals = toeplitz[delta]
                    else:
                        far = 2 * MAX_REL if delta > 0 else 0
                        vals = jnp.full((T, T), rb_ref[h, far], F32)
                    tile = jnp.where(visible, vals * LOG2E, NEG)
                bias_ref[h, qi * T:(qi + 1) * T, ki * T:(ki + 1) * T] = tile


def _attn_kernel(rb_ref, rows_ref, q_ref, *refs):
    k_refs = refs[:NKB]
    v_refs = refs[NKB:2 * NKB]
    out_ref = refs[2 * NKB]
    bias_ref = refs[2 * NKB + 1]
    j = pl.program_id(1)
    nk = NKB * TQ

    @pl.when((pl.program_id(0) == 0) & (j == 0))
    def _():
        _build_band_bias(rb_ref, rows_ref, bias_ref)

    def heads(at_start):
        if at_start:
            kpos = lax.broadcasted_iota(jnp.int32, (1, nk), 1)
            valid = kpos >= (NKB - 1 - j) * TQ
        gw = HG * A_HEAD_DIM
        lane_head = lax.broadcasted_iota(jnp.int32, (1, gw), 1) // A_HEAD_DIM
        onehot = [(lane_head == e).astype(BF16) for e in range(HG)]
        for g in range(A_HEADS // HG):
            gs = slice(g * gw, (g + 1) * gw)
            kq = jnp.concatenate([r[:, gs] for r in k_refs], axis=0)
            vq = jnp.concatenate([r[:, gs] for r in v_refs], axis=0)
            kbd = jnp.concatenate([kq * onehot[e] for e in range(HG)], axis=0)
            vbd = jnp.concatenate([vq * onehot[e] for e in range(HG)], axis=0)
            s = lax.dot_general(q_ref[:, gs], kbd, _NT, preferred_element_type=F32)
            probs, scale = [], None
            for e in range(HG):
                sh = s[:, e * nk:(e + 1) * nk] + bias_ref[g * HG + e]
                if at_start:
                    sh = jnp.where(valid, sh, NEG)
                m = jnp.max(sh, axis=-1, keepdims=True)
                p = jnp.exp2(sh - m)
                rl = 1.0 / jnp.sum(p, axis=-1, keepdims=True)
                probs.append(p.astype(BF16))
                scale = rl if e == 0 else jnp.where(lane_head >= e, rl, scale)
            o = jnp.dot(jnp.concatenate(probs, axis=1), vbd, preferred_element_type=F32) * scale
            out_ref[:, gs] = o.astype(out_ref.dtype)

    pl.when(j < NKB - 1)(functools.partial(heads, True))
    pl.when(j >= NKB - 1)(functools.partial(heads, False))


def _attention(aq, ak, av, rel_bias):
    B, S, W = aq.shape
    blk = lambda back: pl.BlockSpec((None, TQ, W), lambda b, j: (b, jnp.maximum(j - back, 0), 0))
    kv_specs = [blk(NKB - 1 - n) for n in range(NKB)]
    rows = _bias_rows(rel_bias)
    return pl.pallas_call(
        _attn_kernel,
        grid=(B, S // TQ),
        in_specs=[pl.BlockSpec(memory_space=pltpu.SMEM), _const_spec(rows.shape), blk(0)] + kv_specs + kv_specs,
        out_specs=blk(0),
        out_shape=jax.ShapeDtypeStruct((B, S, W), BF16),
        scratch_shapes=[pltpu.VMEM((A_HEADS, TQ, NKB * TQ), F32)],
        compiler_params=pltpu.CompilerParams(
            dimension_semantics=("arbitrary", "arbitrary"), vmem_limit_bytes=VMEM_LIMIT),
        name="band_attn",
    )(rel_bias.astype(F32), rows, aq, *([ak] * NKB), *([av] * NKB))


def _gelu_tanh(x):
    c = 0.7978845608028654
    half = 0.5 * x
    return half * jnp.tanh(x * (c + (c * 0.044715) * (x * x))) + half


def _mixer_kernel(x_ref, mh_ref, ah_ref, p_ref, woa_ref, wob_ref, gffn_ref, wu_ref, cw_ref, cb_ref,
                  wd_ref, gple_ref, wpg_ref, wpp_ref, gfin_ref,
                  out_ref, a_ref, acc_ref, u0_ref, u1_ref, carry_ref):
    tm = x_ref.shape[0]

    @pl.when(pl.program_id(1) == 0)
    def _():
        carry_ref[...] = jnp.zeros(carry_ref.shape, F32)

    h1 = (x_ref[...]
          + jnp.dot(mh_ref[...], woa_ref[...], preferred_element_type=F32)
          + jnp.dot(ah_ref[...], wob_ref[...], preferred_element_type=F32))
    acc_ref[...] = h1
    a_ref[...] = _rms(h1, gffn_ref[...]).astype(BF16)

    def cols(n):
        if isinstance(n, int):
            return slice(n * FC, (n + 1) * FC)
        return pl.ds(pl.multiple_of(n * FC, FC), FC)

    def up(c, u_ref):
        a = a_ref[...]
        u_ref[0, 0:SUBLANES, :] = carry_ref[c, 0]
        u_ref[1, 0:SUBLANES, :] = carry_ref[c, 1]
        u_ref[0, SUBLANES:, :] = jnp.dot(a, wu_ref[:, cols(c)], preferred_element_type=F32)
        u_ref[1, SUBLANES:, :] = jnp.dot(a, wu_ref[:, cols(NCH + c)], preferred_element_type=F32)

    def conv(u_ref, part, w, b):
        y = b + w[FFN_CONV - 1:FFN_CONV, :] * u_ref[part, SUBLANES:, :]
        for k in range(1, FFN_CONV):
            y = y + w[FFN_CONV - 1 - k:FFN_CONV - k, :] * u_ref[part, SUBLANES - k:SUBLANES - k + tm, :]
        return y

    def down(c, u_ref):
        gate = conv(u_ref, 0, cw_ref[:, cols(c)], cb_ref[:, cols(c)])
        val = conv(u_ref, 1, cw_ref[:, cols(NCH + c)], cb_ref[:, cols(NCH + c)])
        carry_ref[c, 0] = u_ref[0, tm:tm + SUBLANES, :]
        carry_ref[c, 1] = u_ref[1, tm:tm + SUBLANES, :]
        act = (_gelu_tanh(gate) * val).astype(BF16)
        acc_ref[...] += jnp.dot(act, wd_ref[c], preferred_element_type=F32)

    u_refs = (u0_ref, u1_ref)

    def step(c, par, last=False):
        if not last:
            up(c + 1, u_refs[1 - par])
        down(c, u_refs[par])

    assert UNROLL % 2 == 0
    up(0, u0_ref)

    def body(i, carry):
        for k in range(UNROLL):
            step(UNROLL * i + k, k % 2)
        return carry

    n_loop = (NCH - 1) // UNROLL
    lax.fori_loop(0, n_loop, body, 0)
    for c in range(n_loop * UNROLL, NCH):
        step(c, c % 2, last=(c == NCH - 1))

    h2 = acc_ref[...]
    gate = _sigmoid(jnp.dot(_rms(h2, gple_ref[...]).astype(BF16), wpg_ref[...], preferred_element_type=F32))
    emb = jnp.dot(p_ref[...].astype(BF16), wpp_ref[...], preferred_element_type=F32)
    h3 = h2 + emb * gate
    out_ref[...] = _rms(h3, gfin_ref[...])


def _mixer(x, mh, ah, p, weights):
    B, S, D = x.shape
    tm = TM_FF
    tok = lambda w: pl.BlockSpec((None, tm, w), lambda b, i: (b, i, 0))
    return pl.pallas_call(
        _mixer_kernel,
        grid=(B, S // tm),
        in_specs=[tok(D), tok(M_WIDTH), tok(A_WIDTH), tok(D_PLE)] + [_const_spec(w.shape) for w in weights],
        out_specs=tok(D),
        out_shape=jax.ShapeDtypeStruct((B, S, D), F32),
        scratch_shapes=[pltpu.VMEM((tm, D), BF16),
                        pltpu.VMEM((tm, D), F32),
                        pltpu.VMEM((2, tm + SUBLANES, FC), F32),
                        pltpu.VMEM((2, tm + SUBLANES, FC), F32),
                        pltpu.VMEM((NCH, 2, SUBLANES, FC), F32)],
        compiler_params=pltpu.CompilerParams(
            dimension_semantics=("arbitrary", "arbitrary"), vmem_limit_bytes=VMEM_LIMIT),
        name="mixer",
    )(x, mh, ah, p, *weights)


def _layer(h, p, g_mix, w_in, b_igate, b_fgate, w_qk_conv, b_qk_conv, g_mhead, rel_bias, w_out,
           g_ffn, w_ffn_up, w_ffn_conv, b_ffn_conv, w_ffn_down, g_ple, w_ple_gate, w_ple_proj, g_out):
    B, S, D = h.shape
    row = lambda v: v.reshape(1, -1).astype(F32)
    m_v = 2 * M_WIDTH
    m_i = 4 * M_WIDTH
    a_q = m_i + 2 * M_HEADS
    wqk = w_in[:, :m_v].astype(BF16)
    wvot = w_in[:, m_v:m_i].T.astype(BF16)
    wgate = w_in[:, m_i:a_q]
    wg = jnp.pad(wgate, ((0, 0), (0, LANES - 2 * M_HEADS))).astype(BF16)
    wgt = wgate.T.astype(BF16)
    watt = w_in[:, a_q:].astype(BF16)
    gbias = jnp.concatenate([b_igate, b_fgate]).astype(F32)
    gb = jnp.pad(gbias, (0, LANES - 2 * M_HEADS)).reshape(1, LANES)
    gbt = gbias.reshape(2 * M_HEADS, 1)

    mq, mk, mvt, mot, gcol, grow, aq, ak, av = _inproj(
        h, row(g_mix), wqk, wvot, wg, wgt, watt, w_qk_conv.astype(F32), row(b_qk_conv), gb, gbt)

    g_mhead_rep = jnp.broadcast_to(g_mhead.astype(F32)[:, None], (M_WIDTH, LANES))
    mh = _mlstm(mq, mk, mvt, mot, gcol, grow, g_mhead_rep)
    ah = _attention(aq, ak, av, rel_bias)

    weights = (
        w_out[:M_WIDTH].astype(BF16), w_out[M_WIDTH:].astype(BF16), row(g_ffn),
        w_ffn_up.astype(BF16), w_ffn_conv.astype(F32), row(b_ffn_conv),
        w_ffn_down.astype(BF16).reshape(NCH, FC, D),
        row(g_ple), w_ple_gate.astype(BF16), w_ple_proj.astype(BF16), g_out,
    )
    return _mixer(h, mh, ah, p, weights)


def kernel(x, p, g_mix, w_in, b_igate, b_fgate, w_qk_conv, b_qk_conv, g_mhead, rel_bias, w_out, g_ffn,
           w_ffn_up, w_ffn_conv, b_ffn_conv, w_ffn_down, g_ple, w_ple_gate, w_ple_proj, g_final):
    depth = w_in.shape[0]
    assert depth == 1, "the final norm is fused into the last layer's channel-mixer kernel"
    i = 0
    return _layer(x, p[i], g_mix[i], w_in[i], b_igate[i], b_fgate[i], w_qk_conv[i], b_qk_conv[i],
                  g_mhead[i], rel_bias[i], w_out[i], g_ffn[i], w_ffn_up[i], w_ffn_conv[i], b_ffn_conv[i],
                  w_ffn_down[i], g_ple[i], w_ple_gate[i], w_ple_proj[i], g_final.reshape(1, -1).astype(F32))
```

```python
import functools

import numpy as np
import jax
import jax.numpy as jnp
from jax import lax
from jax.experimental import pallas as pl
from jax.experimental.pallas import tpu as pltpu

F32 = jnp.float32
BF16 = jnp.bfloat16

D_MODEL = 1024
CHUNK = 64
M_HEADS = 4
M_WIDTH = 512
M_HEAD_DIM = 128
QK_CONV = 4
A_HEADS = 8
A_WIDTH = 512
A_HEAD_DIM = 64
BAND_CHUNKS = 8
MAX_REL = 128
D_FF = 2816
FFN_CONV = 3
D_PLE = 256
EPS = 1e-6

LANES = 128
SUBLANES = 8
NEG = -1e30
LOG2E = 1.4426950408889634

TM_IN = 1024
ML_CHUNK = 128
ML_BLOCK = 1024
TQ = 256
NKB = 1 + (BAND_CHUNKS * CHUNK) // TQ
HG = 2 * LANES // A_HEAD_DIM
Q_SUB = 2
BQ = Q_SUB * TQ
NKBB = 1 + ((NKB - 1) * TQ) // BQ
TM_FF = 512
FC = 256
NCH = D_FF // FC
UNROLL = 2

VMEM_LIMIT = 56 * 1024 * 1024

_NT = (((1,), (1,)), ((), ()))


def _rms(x, g):
    return x * lax.rsqrt(jnp.mean(x * x, axis=-1, keepdims=True) + EPS) * g


def _sigmoid(x):
    return 1.0 / (1.0 + jnp.exp(-x))


def _log_sigmoid(x):
    return jnp.minimum(x, 0.0) - jnp.log1p(jnp.exp(-jnp.abs(x)))


def _split3(x):
    hi = x.astype(BF16)
    r = x - hi.astype(F32)
    mid = r.astype(BF16)
    lo = (r - mid.astype(F32)).astype(BF16)
    return hi, mid, lo


def _const_spec(shape):
    n = len(shape)
    return pl.BlockSpec(shape, lambda *_: (0,) * n, pipeline_mode=pl.Buffered(1))


def _inproj_kernel(x_ref, g_ref, wqk_ref, wvot_ref, wg_ref, wgt_ref, watt_ref, cw_ref, cb_ref,
                   gb_ref, gbt_ref,
                   mq_ref, mk_ref, mvt_ref, mot_ref, gcol_ref, grow_ref, aq_ref, ak_ref, av_ref,
                   zext_ref):
    tm = x_ref.shape[0]

    @pl.when(pl.program_id(1) == 0)
    def _():
        zext_ref[0:SUBLANES, :] = jnp.zeros((SUBLANES, 2 * M_WIDTH), F32)

    a = _rms(x_ref[...], g_ref[...]).astype(BF16)

    zqk = jnp.dot(a, wqk_ref[...], preferred_element_type=F32)
    zext_ref[SUBLANES:, :] = zqk
    cw = cw_ref[...]
    acc = cb_ref[...] + cw[QK_CONV - 1:QK_CONV, :] * zqk
    for k in range(1, QK_CONV):
        acc = acc + cw[QK_CONV - 1 - k:QK_CONV - k, :] * zext_ref[SUBLANES - k:SUBLANES - k + tm, :]
    zext_ref[0:SUBLANES, :] = zext_ref[tm:tm + SUBLANES, :]
    qk = acc * _sigmoid(acc)
    mq_ref[...] = qk[:, :M_WIDTH].astype(BF16)
    mk_ref[...] = (qk[:, M_WIDTH:] * (M_HEAD_DIM ** -0.5)).astype(BF16)

    zvot = lax.dot_general(wvot_ref[...], a, _NT, preferred_element_type=F32)
    mvt_ref[...] = zvot[:M_WIDTH].astype(BF16)
    mot_ref[...] = zvot[M_WIDTH:]

    zg = jnp.dot(a, wg_ref[...], preferred_element_type=F32) + gb_ref[...]
    lane = lax.broadcasted_iota(jnp.int32, zg.shape, 1)
    gcol_ref[...] = jnp.where(lane < M_HEADS, zg, _log_sigmoid(zg))
    zgt = lax.dot_general(wgt_ref[...], a, _NT, preferred_element_type=F32) + gbt_ref[...]
    row = lax.broadcasted_iota(jnp.int32, zgt.shape, 0)
    grow_ref[...] = jnp.where(row < M_HEADS, zgt, _log_sigmoid(zgt))

    zatt = jnp.dot(a, watt_ref[...], preferred_element_type=F32)
    aq_ref[...] = (zatt[:, :A_WIDTH] * (LOG2E * A_HEAD_DIM ** -0.5)).astype(BF16)
    ak_ref[...] = zatt[:, A_WIDTH:2 * A_WIDTH].astype(BF16)
    av_ref[...] = zatt[:, 2 * A_WIDTH:].astype(BF16)


def _inproj(x, g_mix, wqk, wvot, wg, wgt, watt, cw, cb, gb, gbt):
    B, S, D = x.shape
    tm = TM_IN
    tok = lambda w: pl.BlockSpec((None, tm, w), lambda b, i: (b, i, 0))
    feat = lambda w: pl.BlockSpec((None, w, tm), lambda b, i: (b, 0, i))
    out_shape = (
        jax.ShapeDtypeStruct((B, S, M_WIDTH), BF16),
        jax.ShapeDtypeStruct((B, S, M_WIDTH), BF16),
        jax.ShapeDtypeStruct((B, M_WIDTH, S), BF16),
        jax.ShapeDtypeStruct((B, M_WIDTH, S), F32),
        jax.ShapeDtypeStruct((B, S, LANES), F32),
        jax.ShapeDtypeStruct((B, SUBLANES, S), F32),
        jax.ShapeDtypeStruct((B, S, A_WIDTH), BF16),
        jax.ShapeDtypeStruct((B, S, A_WIDTH), BF16),
        jax.ShapeDtypeStruct((B, S, A_WIDTH), BF16),
    )
    out_specs = (tok(M_WIDTH), tok(M_WIDTH), feat(M_WIDTH), feat(M_WIDTH), tok(LANES), feat(SUBLANES),
                 tok(A_WIDTH), tok(A_WIDTH), tok(A_WIDTH))
    consts = (g_mix, wqk, wvot, wg, wgt, watt, cw, cb, gb, gbt)
    return pl.pallas_call(
        _inproj_kernel,
        grid=(B, S // tm),
        in_specs=[tok(D)] + [_const_spec(w.shape) for w in consts],
        out_specs=out_specs,
        out_shape=out_shape,
        scratch_shapes=[pltpu.VMEM((tm + SUBLANES, 2 * M_WIDTH), F32)],
        compiler_params=pltpu.CompilerParams(
            dimension_semantics=("arbitrary", "arbitrary"), vmem_limit_bytes=VMEM_LIMIT),
        name="inproj",
    )(x, *consts)


def _mlstm_consts():
    L = ML_CHUNK
    s = np.arange(L)
    lower = (s[:, None] >= s[None, :]).astype(np.float32)
    upper = lower.T
    sel = np.zeros((LANES, M_HEADS * LANES), np.float32)
    for h in range(M_HEADS):
        sel[h, h * LANES:(h + 1) * LANES] = 1.0
        sel[M_HEADS + h, h * LANES:(h + 1) * LANES] = -1.0
    to_bf16 = lambda m: jnp.asarray(m, BF16)
    return (to_bf16(np.concatenate([lower] * 3, axis=1)),
            to_bf16(np.concatenate([upper] * 3, axis=0)),
            to_bf16(np.concatenate([sel] * 3, axis=0)))


def _mlstm_kernel(q_ref, k_ref, vt_ref, ot_ref, gcol_ref, grow_ref, lower3_ref, upper3_ref, sel3_ref, gh_ref,
                  out_ref, c_ref, n_ref, m_ref):
    L = ML_CHUNK
    dh = M_HEAD_DIM

    @pl.when(pl.program_id(1) == 0)
    def _():
        c_ref[...] = jnp.zeros(c_ref.shape, F32)
        n_ref[...] = jnp.zeros(n_ref.shape, F32)
        m_ref[...] = jnp.zeros(m_ref.shape, F32)

    rr = lax.broadcasted_iota(jnp.int32, (L, L), 0)
    cc = lax.broadcasted_iota(jnp.int32, (L, L), 1)
    causal_t = rr <= cc
    lane = lax.broadcasted_iota(jnp.int32, (L, LANES), 1)
    PR = 2 * SUBLANES

    def blockdiag(x, y):
        zx = jnp.zeros((x.shape[0], y.shape[1]), x.dtype)
        zy = jnp.zeros((y.shape[0], x.shape[1]), y.dtype)
        return jnp.concatenate([jnp.concatenate([x, zx], axis=1), jnp.concatenate([zy, y], axis=1)], axis=0)

    for c in range(q_ref.shape[0] // L):
        rows = slice(c * L, (c + 1) * L)
        gc = gcol_ref[rows, :]
        gr = grow_ref[:, rows]
        bc = jnp.dot(lower3_ref[...], jnp.concatenate(_split3(gc), axis=0), preferred_element_type=F32)
        br = jnp.dot(jnp.concatenate(_split3(gr), axis=1), upper3_ref[...], preferred_element_type=F32)
        ib = jnp.where(lane < M_HEADS, gc, bc)
        a_all = jnp.dot(jnp.concatenate(_split3(ib), axis=1), sel3_ref[...], preferred_element_type=F32)
        for g in range(M_HEADS // 2):
            heads = (2 * g, 2 * g + 1)
            ps = slice(2 * g * dh, (2 * g + 2) * dh)
            qp = q_ref[rows, ps]
            kp = k_ref[rows, ps]
            vtp = vt_ref[ps, rows]
            c_prev = [c_ref[h] for h in heads]
            n_prev = [n_ref[h] for h in heads]
            m_prev = [m_ref[h][0:1, 0:1] for h in heads]

            st = lax.dot_general(kp, blockdiag(qp[:, :dh], qp[:, dh:]), _NT, preferred_element_type=F32)
            n2 = [jnp.concatenate([n, n], axis=0).astype(BF16) for n in n_prev]
            cq = lax.dot_general(
                jnp.concatenate([blockdiag(c_prev[0].astype(BF16), c_prev[1].astype(BF16)),
                                 blockdiag(n2[0], n2[1])], axis=0),
                qp, _NT, preferred_element_type=F32)

            rhs, keep = [], []
            for e, h in enumerate(heads):
                a_c = a_all[:, h * dh:(h + 1) * dh]
                i_r = gr[h:h + 1, :]
                b_r = br[M_HEADS + h:M_HEADS + h + 1, :]
                b_last = b_r[:, L - 1:L]
                dmat = jnp.where(causal_t, a_c + b_r, -jnp.inf)
                inter = b_r + m_prev[e]
                m_t = jnp.maximum(inter, jnp.max(dmat, axis=0, keepdims=True))
                sw = st[:, e * L:(e + 1) * L] * jnp.exp(dmat - m_t)
                g_r = b_last - b_r + i_r
                m_new = jnp.maximum(b_last + m_prev[e], jnp.max(g_r, axis=-1, keepdims=True))
                w_c = jnp.exp(a_c + (b_last - m_new))
                kw = kp[:, e * dh:(e + 1) * dh].astype(F32) * w_c
                rhs.append(jnp.concatenate([sw.astype(BF16), kw.astype(BF16)], axis=1))
                keep.append((m_t, jnp.exp(inter - m_t), jnp.exp(b_last + m_prev[e] - m_new), m_new))

            ones = jnp.ones((PR, L), BF16)
            vs = jnp.dot(
                jnp.concatenate([blockdiag(vtp[:dh], vtp[dh:]), blockdiag(ones, ones)], axis=0),
                jnp.concatenate(rhs, axis=0), preferred_element_type=F32)

            for e, h in enumerate(heads):
                hs = slice(h * dh, (h + 1) * dh)
                m_t, w_inter, decay, m_new = keep[e]
                vr = slice(e * dh, (e + 1) * dh)
                nr = 2 * dh + e * PR
                num = vs[vr, :L] + w_inter * cq[vr, :]
                den = vs[nr:nr + 1, :L] + w_inter * cq[nr:nr + 1, :]
                ht = num * (1.0 / jnp.maximum(jnp.abs(den), jnp.exp(-m_t)))

                mu = jnp.mean(ht, axis=0, keepdims=True)
                dc = ht - mu
                var = jnp.mean(dc * dc, axis=0, keepdims=True)
                yt = dc * lax.rsqrt(var + EPS) * gh_ref[hs, :] * _sigmoid(ot_ref[hs, rows])
                out_ref[rows, hs] = yt.T.astype(out_ref.dtype)

                c_ref[h] = decay * c_prev[e] + vs[vr, L:]
                n_ref[h] = decay * n_prev[e] + vs[nr:nr + SUBLANES, L:]
                m_ref[h] = jnp.broadcast_to(m_new, (SUBLANES, LANES))


def _mlstm(mq, mk, mvt, mot, gcol, grow, g_mhead_rep):
    B, S, W = mq.shape
    tb = ML_BLOCK
    tok = lambda w: pl.BlockSpec((None, tb, w), lambda b, j: (b, j, 0))
    feat = lambda w: pl.BlockSpec((None, w, tb), lambda b, j: (b, 0, j))
    consts = _mlstm_consts() + (g_mhead_rep,)
    return pl.pallas_call(
        _mlstm_kernel,
        grid=(B, S // tb),
        in_specs=[tok(W), tok(W), feat(W), feat(W), tok(LANES), feat(SUBLANES)]
        + [_const_spec(w.shape) for w in consts],
        out_specs=tok(W),
        out_shape=jax.ShapeDtypeStruct((B, S, W), BF16),
        scratch_shapes=[pltpu.VMEM((M_HEADS, M_HEAD_DIM, M_HEAD_DIM), F32),
                        pltpu.VMEM((M_HEADS, SUBLANES, M_HEAD_DIM), F32),
                        pltpu.VMEM((M_HEADS, SUBLANES, LANES), F32)],
        compiler_params=pltpu.CompilerParams(
            dimension_semantics=("arbitrary", "arbitrary"), vmem_limit_bytes=VMEM_LIMIT),
        name="mlstm",
    )(mq, mk, mvt, mot, gcol, grow, *consts)


def _tile_delta(qi, ki):
    return qi - ki + (NKB - 1) * TQ // LANES


def _partial_deltas():
    T = LANES
    deltas = sorted({_tile_delta(qi, ki) for qi in range(TQ // T) for ki in range(NKB * TQ // T)})
    return [d for d in deltas if T * d - (T - 1) < MAX_REL and T * d + (T - 1) > -MAX_REL]


def _bias_rows(rel_bias):
    T = LANES
    x = np.arange(2 * T)
    xs = np.where(x < T, x, x - 2 * T)
    rows = []
    for d in _partial_deltas():
        idx = np.clip(T * d - xs, -MAX_REL, MAX_REL) + MAX_REL
        rows.append(rel_bias[:, idx])
    return jnp.stack(rows, axis=1).astype(F32)


def _build_band_bias(rb_ref, rows_ref, bias_ref):
    T = LANES
    rr = lax.broadcasted_iota(jnp.int32, (T, T), 0)
    cc = lax.broadcasted_iota(jnp.int32, (T, T), 1)
    rr2 = lax.broadcasted_iota(jnp.int32, (T, 2 * T), 0)
    back = (NKB - 1) * TQ
    partial = _partial_deltas()
    for h in range(A_HEADS):
        toeplitz = {}
        for n, delta in enumerate(partial):
            tab = jnp.broadcast_to(rows_ref[h, n:n + 1, :], (T, 2 * T))
            for bit in range(T.bit_length() - 1):
                rolled = pltpu.roll(tab, 1 << bit, 1)
                tab = jnp.where(((rr2 >> bit) & 1) == 1, rolled, tab)
            toeplitz[delta] = tab[:, :T]

        for qi in range(TQ // T):
            for ki in range(NKB * TQ // T):
                qc = (rr + qi * T + back) // CHUNK
                kc = (cc + ki * T) // CHUNK
                visible = (kc <= qc) & (kc >= qc - BAND_CHUNKS)
                q_lo, q_hi = (qi * T + back) // CHUNK, (qi * T + T - 1 + back) // CHUNK
                k_lo, k_hi = (ki * T) // CHUNK, (ki * T + T - 1) // CHUNK
                delta = _tile_delta(qi, ki)
                if k_lo > q_hi or k_hi < q_lo - BAND_CHUNKS:
                    tile = jnp.full((T, T), NEG, F32)
                else:
                    if delta in toeplitz:
                        vals = toeplitz[delta]
                    else:
                        far = 2 * MAX_REL if delta > 0 else 0
                        vals = jnp.full((T, T), rb_ref[h, far], F32)
                    tile = jnp.where(visible, vals * LOG2E, NEG)
                bias_ref[h, qi * T:(qi + 1) * T, ki * T:(ki + 1) * T] = tile


def _attn_kernel(rb_ref, rows_ref, q_ref, *refs):
    k_refs = refs[:NKBB]
    v_refs = refs[NKBB:2 * NKBB]
    out_ref = refs[2 * NKBB]
    bias_ref = refs[2 * NKBB + 1]
    j = pl.program_id(1)
    nk = NKB * TQ
    back = (NKB - 1) * TQ
    assert back % BQ == 0

    def window(blocks, u, gs):
        lo, hi = u * TQ, u * TQ + nk
        pieces = []
        for n, r in enumerate(blocks):
            a, b = max(lo, n * BQ), min(hi, (n + 1) * BQ)
            if a < b:
                pieces.append(r[a - n * BQ:b - n * BQ, gs])
        return jnp.concatenate(pieces, axis=0)

    @pl.when((pl.program_id(0) == 0) & (j == 0))
    def _():
        _build_band_bias(rb_ref, rows_ref, bias_ref)

    def heads(at_start):
        for u in range(Q_SUB):
            query_block(u, at_start)

    def query_block(u, at_start):
        qs = slice(u * TQ, (u + 1) * TQ)
        if at_start:
            kpos = lax.broadcasted_iota(jnp.int32, (1, nk), 1)
            valid = kpos >= back - u * TQ - j * BQ
        gw = HG * A_HEAD_DIM
        lane_head = lax.broadcasted_iota(jnp.int32, (1, gw), 1) // A_HEAD_DIM
        onehot = [(lane_head == e).astype(BF16) for e in range(HG)]
        for g in range(A_HEADS // HG):
            gs = slice(g * gw, (g + 1) * gw)
            kq = window(k_refs, u, gs)
            vq = window(v_refs, u, gs)
            kbd = jnp.concatenate([kq * onehot[e] for e in range(HG)], axis=0)
            vbd = jnp.concatenate([vq * onehot[e] for e in range(HG)], axis=0)
            s = lax.dot_general(q_ref[qs, gs], kbd, _NT, preferred_element_type=F32)
            probs, scale = [], None
            for e in range(HG):
                sh = s[:, e * nk:(e + 1) * nk] + bias_ref[g * HG + e]
                if at_start:
                    sh = jnp.where(valid, sh, NEG)
                m = jnp.max(sh, axis=-1, keepdims=True)
                p = jnp.exp2(sh - m)
                rl = 1.0 / jnp.sum(p, axis=-1, keepdims=True)
                probs.append(p.astype(BF16))
                scale = rl if e == 0 else jnp.where(lane_head >= e, rl, scale)
            o = jnp.dot(jnp.concatenate(probs, axis=1), vbd, preferred_element_type=F32) * scale
            out_ref[qs, gs] = o.astype(out_ref.dtype)

    pl.when(j * BQ < back)(functools.partial(heads, True))
    pl.when(j * BQ >= back)(functools.partial(heads, False))


def _attention(aq, ak, av, rel_bias):
    B, S, W = aq.shape
    blk = lambda back: pl.BlockSpec((None, BQ, W), lambda b, j: (b, jnp.maximum(j - back, 0), 0))
    kv_specs = [blk(NKBB - 1 - n) for n in range(NKBB)]
    rows = _bias_rows(rel_bias)
    return pl.pallas_call(
        _attn_kernel,
        grid=(B, S // BQ),
        in_specs=[pl.BlockSpec(memory_space=pltpu.SMEM), _const_spec(rows.shape), blk(0)] + kv_specs + kv_specs,
        out_specs=blk(0),
        out_shape=jax.ShapeDtypeStruct((B, S, W), BF16),
        scratch_shapes=[pltpu.VMEM((A_HEADS, TQ, NKB * TQ), F32)],
        compiler_params=pltpu.CompilerParams(
            dimension_semantics=("arbitrary", "arbitrary"), vmem_limit_bytes=VMEM_LIMIT),
        name="band_attn",
    )(rel_bias.astype(F32), rows, aq, *([ak] * NKBB), *([av] * NKBB))


def _gelu_tanh(x):
    c = 0.7978845608028654
    half = 0.5 * x
    return half * jnp.tanh(x * (c + (c * 0.044715) * (x * x))) + half


def _mixer_kernel(x_ref, mh_ref, ah_ref, p_ref, woa_ref, wob_ref, gffn_ref, wu_ref, cw_ref, cb_ref,
                  wd_ref, gple_ref, wpg_ref, wpp_ref, gfin_ref,
                  out_ref, a_ref, acc_ref, u0_ref, u1_ref, carry_ref):
    tm = x_ref.shape[0]

    @pl.when(pl.program_id(1) == 0)
    def _():
        carry_ref[...] = jnp.zeros(carry_ref.shape, F32)

    h1 = (x_ref[...]
          + jnp.dot(mh_ref[...], woa_ref[...], preferred_element_type=F32)
          + jnp.dot(ah_ref[...], wob_ref[...], preferred_element_type=F32))
    acc_ref[...] = h1
    a_ref[...] = _rms(h1, gffn_ref[...]).astype(BF16)

    def cols(n):
        if isinstance(n, int):
            return slice(n * FC, (n + 1) * FC)
        return pl.ds(pl.multiple_of(n * FC, FC), FC)

    def up(c, u_ref):
        a = a_ref[...]
        u_ref[0, 0:SUBLANES, :] = carry_ref[c, 0]
        u_ref[1, 0:SUBLANES, :] = carry_ref[c, 1]
        u_ref[0, SUBLANES:, :] = jnp.dot(a, wu_ref[:, cols(c)], preferred_element_type=F32)
        u_ref[1, SUBLANES:, :] = jnp.dot(a, wu_ref[:, cols(NCH + c)], preferred_element_type=F32)

    def conv(u_ref, part, w, b):
        y = b + w[FFN_CONV - 1:FFN_CONV, :] * u_ref[part, SUBLANES:, :]
        for k in range(1, FFN_CONV):
            y = y + w[FFN_CONV - 1 - k:FFN_CONV - k, :] * u_ref[part, SUBLANES - k:SUBLANES - k + tm, :]
        return y

    def down(c, u_ref):
        gate = conv(u_ref, 0, cw_ref[:, cols(c)], cb_ref[:, cols(c)])
        val = conv(u_ref, 1, cw_ref[:, cols(NCH + c)], cb_ref[:, cols(NCH + c)])
        carry_ref[c, 0] = u_ref[0, tm:tm + SUBLANES, :]
        carry_ref[c, 1] = u_ref[1, tm:tm + SUBLANES, :]
        act = (_gelu_tanh(gate) * val).astype(BF16)
        acc_ref[...] += jnp.dot(act, wd_ref[c], preferred_element_type=F32)

    u_refs = (u0_ref, u1_ref)

    def step(c, par, last=False):
        if not last:
            up(c + 1, u_refs[1 - par])
        down(c, u_refs[par])

    assert UNROLL % 2 == 0
    up(0, u0_ref)

    def body(i, carry):
        for k in range(UNROLL):
            step(UNROLL * i + k, k % 2)
        return carry

    n_loop = (NCH - 1) // UNROLL
    lax.fori_loop(0, n_loop, body, 0)
    for c in range(n_loop * UNROLL, NCH):
        step(c, c % 2, last=(c == NCH - 1))

    h2 = acc_ref[...]
    gate = _sigmoid(jnp.dot(_rms(h2, gple_ref[...]).astype(BF16), wpg_ref[...], preferred_element_type=F32))
    emb = jnp.dot(p_ref[...].astype(BF16), wpp_ref[...], preferred_element_type=F32)
    h3 = h2 + emb * gate
    out_ref[...] = _rms(h3, gfin_ref[...])


def _mixer(x, mh, ah, p, weights):
    B, S, D = x.shape
    tm = TM_FF
    tok = lambda w: pl.BlockSpec((None, tm, w), lambda b, i: (b, i, 0))
    return pl.pallas_call(
        _mixer_kernel,
        grid=(B, S // tm),
        in_specs=[tok(D), tok(M_WIDTH), tok(A_WIDTH), tok(D_PLE)] + [_const_spec(w.shape) for w in weights],
        out_specs=tok(D),
        out_shape=jax.ShapeDtypeStruct((B, S, D), F32),
        scratch_shapes=[pltpu.VMEM((tm, D), BF16),
                        pltpu.VMEM((tm, D), F32),
                        pltpu.VMEM((2, tm + SUBLANES, FC), F32),
                        pltpu.VMEM((2, tm + SUBLANES, FC), F32),
                        pltpu.VMEM((NCH, 2, SUBLANES, FC), F32)],
        compiler_params=pltpu.CompilerParams(
            dimension_semantics=("arbitrary", "arbitrary"), vmem_limit_bytes=VMEM_LIMIT),
        name="mixer",
    )(x, mh, ah, p, *weights)


def _layer(h, p, g_mix, w_in, b_igate, b_fgate, w_qk_conv, b_qk_conv, g_mhead, rel_bias, w_out,
           g_ffn, w_ffn_up, w_ffn_conv, b_ffn_conv, w_ffn_down, g_ple, w_ple_gate, w_ple_proj, g_out):
    B, S, D = h.shape
    row = lambda v: v.reshape(1, -1).astype(F32)
    m_v = 2 * M_WIDTH
    m_i = 4 * M_WIDTH
    a_q = m_i + 2 * M_HEADS
    wqk = w_in[:, :m_v].astype(BF16)
    wvot = w_in[:, m_v:m_i].T.astype(BF16)
    wgate = w_in[:, m_i:a_q]
    wg = jnp.pad(wgate, ((0, 0), (0, LANES - 2 * M_HEADS))).astype(BF16)
    wgt = wgate.T.astype(BF16)
    watt = w_in[:, a_q:].astype(BF16)
    gbias = jnp.concatenate([b_igate, b_fgate]).astype(F32)
    gb = jnp.pad(gbias, (0, LANES - 2 * M_HEADS)).reshape(1, LANES)
    gbt = gbias.reshape(2 * M_HEADS, 1)

    mq, mk, mvt, mot, gcol, grow, aq, ak, av = _inproj(
        h, row(g_mix), wqk, wvot, wg, wgt, watt, w_qk_conv.astype(F32), row(b_qk_conv), gb, gbt)

    g_mhead_rep = jnp.broadcast_to(g_mhead.astype(F32)[:, None], (M_WIDTH, LANES))
    mh = _mlstm(mq, mk, mvt, mot, gcol, grow, g_mhead_rep)
    ah = _attention(aq, ak, av, rel_bias)

    weights = (
        w_out[:M_WIDTH].astype(BF16), w_out[M_WIDTH:].astype(BF16), row(g_ffn),
        w_ffn_up.astype(BF16), w_ffn_conv.astype(F32), row(b_ffn_conv),
        w_ffn_down.astype(BF16).reshape(NCH, FC, D),
        row(g_ple), w_ple_gate.astype(BF16), w_ple_proj.astype(BF16), g_out,
    )
    return _mixer(h, mh, ah, p, weights)


def kernel(x, p, g_mix, w_in, b_igate, b_fgate, w_qk_conv, b_qk_conv, g_mhead, rel_bias, w_out, g_ffn,
           w_ffn_up, w_ffn_conv, b_ffn_conv, w_ffn_down, g_ple, w_ple_gate, w_ple_proj, g_final):
    depth = w_in.shape[0]
    assert depth == 1, "the final norm is fused into the last layer's channel-mixer kernel"
    i = 0
    return _layer(x, p[i], g_mix[i], w_in[i], b_igate[i], b_fgate[i], w_qk_conv[i], b_qk_conv[i],
                  g_mhead[i], rel_bias[i], w_out[i], g_ffn[i], w_ffn_up[i], w_ffn_conv[i], b_ffn_conv[i],
                  w_ffn_down[i], g_ple[i], w_ple_gate[i], w_ple_proj[i], g_final.reshape(1, -1).astype(F32))
```

```python
import functools

import numpy as np
import jax
import jax.numpy as jnp
from jax import lax
from jax.experimental import pallas as pl
from jax.experimental.pallas import tpu as pltpu

F32 = jnp.float32
BF16 = jnp.bfloat16

D_MODEL = 1024
CHUNK = 64
M_HEADS = 4
M_WIDTH = 512
M_HEAD_DIM = 128
QK_CONV = 4
A_HEADS = 8
A_WIDTH = 512
A_HEAD_DIM = 64
BAND_CHUNKS = 8
MAX_REL = 128
D_FF = 2816
FFN_CONV = 3
D_PLE = 256
EPS = 1e-6

LANES = 128
SUBLANES = 8
NEG = -1e30
LOG2E = 1.4426950408889634

TM_IN = 1024
ML_CHUNK = 128
ML_BLOCK = 2048
TQ = 256
NKB = 1 + (BAND_CHUNKS * CHUNK) // TQ
HG = 2 * LANES // A_HEAD_DIM
Q_SUB = 4
BQ = Q_SUB * TQ
NKBB = 1 + -(-(NKB - 1) * TQ // BQ)
TM_FF = 512
FC = 256
NCH = D_FF // FC
UNROLL = 2

VMEM_LIMIT = 56 * 1024 * 1024

_NT = (((1,), (1,)), ((), ()))


def _rms(x, g):
    return x * lax.rsqrt(jnp.mean(x * x, axis=-1, keepdims=True) + EPS) * g


def _sigmoid(x):
    return 1.0 / (1.0 + jnp.exp(-x))


def _log_sigmoid(x):
    return jnp.minimum(x, 0.0) - jnp.log1p(jnp.exp(-jnp.abs(x)))


def _split3(x):
    hi = x.astype(BF16)
    r = x - hi.astype(F32)
    mid = r.astype(BF16)
    lo = (r - mid.astype(F32)).astype(BF16)
    return hi, mid, lo


def _const_spec(shape):
    n = len(shape)
    return pl.BlockSpec(shape, lambda *_: (0,) * n, pipeline_mode=pl.Buffered(1))


def _inproj_kernel(x_ref, g_ref, wqk_ref, wvot_ref, wg_ref, wgt_ref, watt_ref, cw_ref, cb_ref,
                   gb_ref, gbt_ref,
                   mq_ref, mk_ref, mvt_ref, mot_ref, gcol_ref, grow_ref, aq_ref, ak_ref, av_ref,
                   zext_ref):
    tm = x_ref.shape[0]

    @pl.when(pl.program_id(1) == 0)
    def _():
        zext_ref[0:SUBLANES, :] = jnp.zeros((SUBLANES, 2 * M_WIDTH), F32)

    a = _rms(x_ref[...], g_ref[...]).astype(BF16)

    zqk = jnp.dot(a, wqk_ref[...], preferred_element_type=F32)
    zext_ref[SUBLANES:, :] = zqk
    cw = cw_ref[...]
    acc = cb_ref[...] + cw[QK_CONV - 1:QK_CONV, :] * zqk
    for k in range(1, QK_CONV):
        acc = acc + cw[QK_CONV - 1 - k:QK_CONV - k, :] * zext_ref[SUBLANES - k:SUBLANES - k + tm, :]
    zext_ref[0:SUBLANES, :] = zext_ref[tm:tm + SUBLANES, :]
    qk = acc * _sigmoid(acc)
    mq_ref[...] = qk[:, :M_WIDTH].astype(BF16)
    mk_ref[...] = (qk[:, M_WIDTH:] * (M_HEAD_DIM ** -0.5)).astype(BF16)

    zvot = lax.dot_general(wvot_ref[...], a, _NT, preferred_element_type=F32)
    mvt_ref[...] = zvot[:M_WIDTH].astype(BF16)
    mot_ref[...] = zvot[M_WIDTH:]

    zg = jnp.dot(a, wg_ref[...], preferred_element_type=F32) + gb_ref[...]
    lane = lax.broadcasted_iota(jnp.int32, zg.shape, 1)
    gcol_ref[...] = jnp.where(lane < M_HEADS, zg, _log_sigmoid(zg))
    zgt = lax.dot_general(wgt_ref[...], a, _NT, preferred_element_type=F32) + gbt_ref[...]
    row = lax.broadcasted_iota(jnp.int32, zgt.shape, 0)
    grow_ref[...] = jnp.where(row < M_HEADS, zgt, _log_sigmoid(zgt))

    zatt = jnp.dot(a, watt_ref[...], preferred_element_type=F32)
    aq_ref[...] = (zatt[:, :A_WIDTH] * (LOG2E * A_HEAD_DIM ** -0.5)).astype(BF16)
    ak_ref[...] = zatt[:, A_WIDTH:2 * A_WIDTH].astype(BF16)
    av_ref[...] = zatt[:, 2 * A_WIDTH:].astype(BF16)


def _inproj(x, g_mix, wqk, wvot, wg, wgt, watt, cw, cb, gb, gbt):
    B, S, D = x.shape
    tm = TM_IN
    tok = lambda w: pl.BlockSpec((None, tm, w), lambda b, i: (b, i, 0))
    feat = lambda w: pl.BlockSpec((None, w, tm), lambda b, i: (b, 0, i))
    out_shape = (
        jax.ShapeDtypeStruct((B, S, M_WIDTH), BF16),
        jax.ShapeDtypeStruct((B, S, M_WIDTH), BF16),
        jax.ShapeDtypeStruct((B, M_WIDTH, S), BF16),
        jax.ShapeDtypeStruct((B, M_WIDTH, S), F32),
        jax.ShapeDtypeStruct((B, S, LANES), F32),
        jax.ShapeDtypeStruct((B, SUBLANES, S), F32),
        jax.ShapeDtypeStruct((B, S, A_WIDTH), BF16),
        jax.ShapeDtypeStruct((B, S, A_WIDTH), BF16),
        jax.ShapeDtypeStruct((B, S, A_WIDTH), BF16),
    )
    out_specs = (tok(M_WIDTH), tok(M_WIDTH), feat(M_WIDTH), feat(M_WIDTH), tok(LANES), feat(SUBLANES),
                 tok(A_WIDTH), tok(A_WIDTH), tok(A_WIDTH))
    consts = (g_mix, wqk, wvot, wg, wgt, watt, cw, cb, gb, gbt)
    return pl.pallas_call(
        _inproj_kernel,
        grid=(B, S // tm),
        in_specs=[tok(D)] + [_const_spec(w.shape) for w in consts],
        out_specs=out_specs,
        out_shape=out_shape,
        scratch_shapes=[pltpu.VMEM((tm + SUBLANES, 2 * M_WIDTH), F32)],
        compiler_params=pltpu.CompilerParams(
            dimension_semantics=("arbitrary", "arbitrary"), vmem_limit_bytes=VMEM_LIMIT),
        name="inproj",
    )(x, *consts)


def _mlstm_consts():
    L = ML_CHUNK
    s = np.arange(L)
    lower = (s[:, None] >= s[None, :]).astype(np.float32)
    upper = lower.T
    sel = np.zeros((LANES, M_HEADS * LANES), np.float32)
    for h in range(M_HEADS):
        sel[h, h * LANES:(h + 1) * LANES] = 1.0
        sel[M_HEADS + h, h * LANES:(h + 1) * LANES] = -1.0
    to_bf16 = lambda m: jnp.asarray(m, BF16)
    return (to_bf16(np.concatenate([lower] * 3, axis=1)),
            to_bf16(np.concatenate([upper] * 3, axis=0)),
            to_bf16(np.concatenate([sel] * 3, axis=0)))


def _mlstm_kernel(q_ref, k_ref, vt_ref, ot_ref, gcol_ref, grow_ref, lower3_ref, upper3_ref, sel3_ref, gh_ref,
                  out_ref, c_ref, n_ref, m_ref):
    L = ML_CHUNK
    dh = M_HEAD_DIM

    @pl.when(pl.program_id(1) == 0)
    def _():
        c_ref[...] = jnp.zeros(c_ref.shape, F32)
        n_ref[...] = jnp.zeros(n_ref.shape, F32)
        m_ref[...] = jnp.zeros(m_ref.shape, F32)

    rr = lax.broadcasted_iota(jnp.int32, (L, L), 0)
    cc = lax.broadcasted_iota(jnp.int32, (L, L), 1)
    causal_t = rr <= cc
    lane = lax.broadcasted_iota(jnp.int32, (L, LANES), 1)
    PR = 2 * SUBLANES

    def blockdiag(x, y):
        zx = jnp.zeros((x.shape[0], y.shape[1]), x.dtype)
        zy = jnp.zeros((y.shape[0], x.shape[1]), y.dtype)
        return jnp.concatenate([jnp.concatenate([x, zx], axis=1), jnp.concatenate([zy, y], axis=1)], axis=0)

    for c in range(q_ref.shape[0] // L):
        rows = slice(c * L, (c + 1) * L)
        gc = gcol_ref[rows, :]
        gr = grow_ref[:, rows]
        bc = jnp.dot(lower3_ref[...], jnp.concatenate(_split3(gc), axis=0), preferred_element_type=F32)
        br = jnp.dot(jnp.concatenate(_split3(gr), axis=1), upper3_ref[...], preferred_element_type=F32)
        ib = jnp.where(lane < M_HEADS, gc, bc)
        a_all = jnp.dot(jnp.concatenate(_split3(ib), axis=1), sel3_ref[...], preferred_element_type=F32)
        for g in range(M_HEADS // 2):
            heads = (2 * g, 2 * g + 1)
            ps = slice(2 * g * dh, (2 * g + 2) * dh)
            qp = q_ref[rows, ps]
            kp = k_ref[rows, ps]
            vtp = vt_ref[ps, rows]
            c_prev = [c_ref[h] for h in heads]
            n_prev = [n_ref[h] for h in heads]
            m_prev = [m_ref[h][0:1, 0:1] for h in heads]

            st = lax.dot_general(kp, blockdiag(qp[:, :dh], qp[:, dh:]), _NT, preferred_element_type=F32)
            n2 = [jnp.concatenate([n, n], axis=0).astype(BF16) for n in n_prev]
            cq = lax.dot_general(
                jnp.concatenate([blockdiag(c_prev[0].astype(BF16), c_prev[1].astype(BF16)),
                                 blockdiag(n2[0], n2[1])], axis=0),
                qp, _NT, preferred_element_type=F32)

            rhs, keep = [], []
            for e, h in enumerate(heads):
                a_c = a_all[:, h * dh:(h + 1) * dh]
                i_r = gr[h:h + 1, :]
                b_r = br[M_HEADS + h:M_HEADS + h + 1, :]
                b_last = b_r[:, L - 1:L]
                dmat = jnp.where(causal_t, a_c + b_r, -jnp.inf)
                inter = b_r + m_prev[e]
                m_t = jnp.maximum(inter, jnp.max(dmat, axis=0, keepdims=True))
                sw = st[:, e * L:(e + 1) * L] * jnp.exp(dmat - m_t)
                g_r = b_last - b_r + i_r
                m_new = jnp.maximum(b_last + m_prev[e], jnp.max(g_r, axis=-1, keepdims=True))
                w_c = jnp.exp(a_c + (b_last - m_new))
                kw = kp[:, e * dh:(e + 1) * dh].astype(F32) * w_c
                rhs.append(jnp.concatenate([sw.astype(BF16), kw.astype(BF16)], axis=1))
                keep.append((m_t, jnp.exp(inter - m_t), jnp.exp(b_last + m_prev[e] - m_new), m_new))

            ones = jnp.ones((PR, L), BF16)
            vs = jnp.dot(
                jnp.concatenate([blockdiag(vtp[:dh], vtp[dh:]), blockdiag(ones, ones)], axis=0),
                jnp.concatenate(rhs, axis=0), preferred_element_type=F32)

            for e, h in enumerate(heads):
                hs = slice(h * dh, (h + 1) * dh)
                m_t, w_inter, decay, m_new = keep[e]
                vr = slice(e * dh, (e + 1) * dh)
                nr = 2 * dh + e * PR
                num = vs[vr, :L] + w_inter * cq[vr, :]
                den = vs[nr:nr + 1, :L] + w_inter * cq[nr:nr + 1, :]
                ht = num * (1.0 / jnp.maximum(jnp.abs(den), jnp.exp(-m_t)))

                mu = jnp.mean(ht, axis=0, keepdims=True)
                dc = ht - mu
                var = jnp.mean(dc * dc, axis=0, keepdims=True)
                yt = dc * lax.rsqrt(var + EPS) * gh_ref[hs, :] * _sigmoid(ot_ref[hs, rows])
                out_ref[rows, hs] = yt.T.astype(out_ref.dtype)

                c_ref[h] = decay * c_prev[e] + vs[vr, L:]
                n_ref[h] = decay * n_prev[e] + vs[nr:nr + SUBLANES, L:]
                m_ref[h] = jnp.broadcast_to(m_new, (SUBLANES, LANES))


def _mlstm(mq, mk, mvt, mot, gcol, grow, g_mhead_rep):
    B, S, W = mq.shape
    tb = ML_BLOCK
    tok = lambda w: pl.BlockSpec((None, tb, w), lambda b, j: (b, j, 0))
    feat = lambda w: pl.BlockSpec((None, w, tb), lambda b, j: (b, 0, j))
    consts = _mlstm_consts() + (g_mhead_rep,)
    return pl.pallas_call(
        _mlstm_kernel,
        grid=(B, S // tb),
        in_specs=[tok(W), tok(W), feat(W), feat(W), tok(LANES), feat(SUBLANES)]
        + [_const_spec(w.shape) for w in consts],
        out_specs=tok(W),
        out_shape=jax.ShapeDtypeStruct((B, S, W), BF16),
        scratch_shapes=[pltpu.VMEM((M_HEADS, M_HEAD_DIM, M_HEAD_DIM), F32),
                        pltpu.VMEM((M_HEADS, SUBLANES, M_HEAD_DIM), F32),
                        pltpu.VMEM((M_HEADS, SUBLANES, LANES), F32)],
        compiler_params=pltpu.CompilerParams(
            dimension_semantics=("arbitrary", "arbitrary"), vmem_limit_bytes=VMEM_LIMIT),
        name="mlstm",
    )(mq, mk, mvt, mot, gcol, grow, *consts)


def _tile_delta(qi, ki):
    return qi - ki + (NKB - 1) * TQ // LANES


def _partial_deltas():
    T = LANES
    deltas = sorted({_tile_delta(qi, ki) for qi in range(TQ // T) for ki in range(NKB * TQ // T)})
    return [d for d in deltas if T * d - (T - 1) < MAX_REL and T * d + (T - 1) > -MAX_REL]


def _bias_rows(rel_bias):
    T = LANES
    x = np.arange(2 * T)
    xs = np.where(x < T, x, x - 2 * T)
    rows = []
    for d in _partial_deltas():
        idx = np.clip(T * d - xs, -MAX_REL, MAX_REL) + MAX_REL
        rows.append(rel_bias[:, idx])
    return jnp.stack(rows, axis=1).astype(F32)


def _build_band_bias(rb_ref, rows_ref, bias_ref):
    T = LANES
    rr = lax.broadcasted_iota(jnp.int32, (T, T), 0)
    cc = lax.broadcasted_iota(jnp.int32, (T, T), 1)
    rr2 = lax.broadcasted_iota(jnp.int32, (T, 2 * T), 0)
    back = (NKB - 1) * TQ
    partial = _partial_deltas()
    for h in range(A_HEADS):
        toeplitz = {}
        for n, delta in enumerate(partial):
            tab = jnp.broadcast_to(rows_ref[h, n:n + 1, :], (T, 2 * T))
            for bit in range(T.bit_length() - 1):
                rolled = pltpu.roll(tab, 1 << bit, 1)
                tab = jnp.where(((rr2 >> bit) & 1) == 1, rolled, tab)
            toeplitz[delta] = tab[:, :T]

        for qi in range(TQ // T):
            for ki in range(NKB * TQ // T):
                qc = (rr + qi * T + back) // CHUNK
                kc = (cc + ki * T) // CHUNK
                visible = (kc <= qc) & (kc >= qc - BAND_CHUNKS)
                q_lo, q_hi = (qi * T + back) // CHUNK, (qi * T + T - 1 + back) // CHUNK
                k_lo, k_hi = (ki * T) // CHUNK, (ki * T + T - 1) // CHUNK
                delta = _tile_delta(qi, ki)
                if k_lo > q_hi or k_hi < q_lo - BAND_CHUNKS:
                    tile = jnp.full((T, T), NEG, F32)
                else:
                    if delta in toeplitz:
                        vals = toeplitz[delta]
                    else:
                        far = 2 * MAX_REL if delta > 0 else 0
                        vals = jnp.full((T, T), rb_ref[h, far], F32)
                    tile = jnp.where(visible, vals * LOG2E, NEG)
                bias_ref[h, qi * T:(qi + 1) * T, ki * T:(ki + 1) * T] = tile


def _attn_kernel(rb_ref, rows_ref, q_ref, *refs):
    k_refs = refs[:NKBB]
    v_refs = refs[NKBB:2 * NKBB]
    out_ref = refs[2 * NKBB]
    bias_ref = refs[2 * NKBB + 1]
    j = pl.program_id(1)
    nk = NKB * TQ
    back = (NKB - 1) * TQ
    first = (NKBB - 1) * BQ - back

    def window(blocks, u, gs):
        lo, hi = first + u * TQ, first + u * TQ + nk
        pieces = []
        for n, r in enumerate(blocks):
            a, b = max(lo, n * BQ), min(hi, (n + 1) * BQ)
            if a < b:
                pieces.append(r[a - n * BQ:b - n * BQ, gs])
        return jnp.concatenate(pieces, axis=0)

    @pl.when((pl.program_id(0) == 0) & (j == 0))
    def _():
        _build_band_bias(rb_ref, rows_ref, bias_ref)

    def heads(at_start):
        for u in range(Q_SUB):
            query_block(u, at_start)

    def query_block(u, at_start):
        qs = slice(u * TQ, (u + 1) * TQ)
        if at_start:
            kpos = lax.broadcasted_iota(jnp.int32, (1, nk), 1)
            valid = kpos >= back - u * TQ - j * BQ
        gw = HG * A_HEAD_DIM
        lane_head = lax.broadcasted_iota(jnp.int32, (1, gw), 1) // A_HEAD_DIM
        onehot = [(lane_head == e).astype(BF16) for e in range(HG)]
        for g in range(A_HEADS // HG):
            gs = slice(g * gw, (g + 1) * gw)
            kq = window(k_refs, u, gs)
            vq = window(v_refs, u, gs)
            kbd = jnp.concatenate([kq * onehot[e] for e in range(HG)], axis=0)
            vbd = jnp.concatenate([vq * onehot[e] for e in range(HG)], axis=0)
            s = lax.dot_general(q_ref[qs, gs], kbd, _NT, preferred_element_type=F32)
            probs, scale = [], None
            for e in range(HG):
                sh = s[:, e * nk:(e + 1) * nk] + bias_ref[g * HG + e]
                if at_start:
                    sh = jnp.where(valid, sh, NEG)
                m = jnp.max(sh, axis=-1, keepdims=True)
                p = jnp.exp2(sh - m)
                rl = 1.0 / jnp.sum(p, axis=-1, keepdims=True)
                probs.append(p.astype(BF16))
                scale = rl if e == 0 else jnp.where(lane_head >= e, rl, scale)
            o = jnp.dot(jnp.concatenate(probs, axis=1), vbd, preferred_element_type=F32) * scale
            out_ref[qs, gs] = o.astype(out_ref.dtype)

    pl.when(j * BQ < back)(functools.partial(heads, True))
    pl.when(j * BQ >= back)(functools.partial(heads, False))


def _attention(aq, ak, av, rel_bias):
    B, S, W = aq.shape
    blk = lambda back: pl.BlockSpec((None, BQ, W), lambda b, j: (b, jnp.maximum(j - back, 0), 0))
    kv_specs = [blk(NKBB - 1 - n) for n in range(NKBB)]
    rows = _bias_rows(rel_bias)
    return pl.pallas_call(
        _attn_kernel,
        grid=(B, S // BQ),
        in_specs=[pl.BlockSpec(memory_space=pltpu.SMEM), _const_spec(rows.shape), blk(0)] + kv_specs + kv_specs,
        out_specs=blk(0),
        out_shape=jax.ShapeDtypeStruct((B, S, W), BF16),
        scratch_shapes=[pltpu.VMEM((A_HEADS, TQ, NKB * TQ), F32)],
        compiler_params=pltpu.CompilerParams(
            dimension_semantics=("arbitrary", "arbitrary"), vmem_limit_bytes=VMEM_LIMIT),
        name="band_attn",
    )(rel_bias.astype(F32), rows, aq, *([ak] * NKBB), *([av] * NKBB))


def _gelu_tanh(x):
    c = 0.7978845608028654
    half = 0.5 * x
    return half * jnp.tanh(x * (c + (c * 0.044715) * (x * x))) + half


def _mixer_kernel(x_ref, mh_ref, ah_ref, p_ref, woa_ref, wob_ref, gffn_ref, wu_ref, cw_ref, cb_ref,
                  wd_ref, gple_ref, wpg_ref, wpp_ref, gfin_ref,
                  out_ref, a_ref, acc_ref, u0_ref, u1_ref, carry_ref):
    tm = x_ref.shape[0]

    @pl.when(pl.program_id(1) == 0)
    def _():
        carry_ref[...] = jnp.zeros(carry_ref.shape, F32)

    h1 = (x_ref[...]
          + jnp.dot(mh_ref[...], woa_ref[...], preferred_element_type=F32)
          + jnp.dot(ah_ref[...], wob_ref[...], preferred_element_type=F32))
    acc_ref[...] = h1
    a_ref[...] = _rms(h1, gffn_ref[...]).astype(BF16)

    def cols(n):
        if isinstance(n, int):
            return slice(n * FC, (n + 1) * FC)
        return pl.ds(pl.multiple_of(n * FC, FC), FC)

    def up(c, u_ref):
        a = a_ref[...]
        u_ref[0, 0:SUBLANES, :] = carry_ref[c, 0]
        u_ref[1, 0:SUBLANES, :] = carry_ref[c, 1]
        u_ref[0, SUBLANES:, :] = jnp.dot(a, wu_ref[:, cols(c)], preferred_element_type=F32)
        u_ref[1, SUBLANES:, :] = jnp.dot(a, wu_ref[:, cols(NCH + c)], preferred_element_type=F32)

    def conv(u_ref, part, w, b):
        y = b + w[FFN_CONV - 1:FFN_CONV, :] * u_ref[part, SUBLANES:, :]
        for k in range(1, FFN_CONV):
            y = y + w[FFN_CONV - 1 - k:FFN_CONV - k, :] * u_ref[part, SUBLANES - k:SUBLANES - k + tm, :]
        return y

    def down(c, u_ref):
        gate = conv(u_ref, 0, cw_ref[:, cols(c)], cb_ref[:, cols(c)])
        val = conv(u_ref, 1, cw_ref[:, cols(NCH + c)], cb_ref[:, cols(NCH + c)])
        carry_ref[c, 0] = u_ref[0, tm:tm + SUBLANES, :]
        carry_ref[c, 1] = u_ref[1, tm:tm + SUBLANES, :]
        act = (_gelu_tanh(gate) * val).astype(BF16)
        acc_ref[...] += jnp.dot(act, wd_ref[c], preferred_element_type=F32)

    u_refs = (u0_ref, u1_ref)

    def step(c, par, last=False):
        if not last:
            up(c + 1, u_refs[1 - par])
        down(c, u_refs[par])

    assert UNROLL % 2 == 0
    up(0, u0_ref)

    def body(i, carry):
        for k in range(UNROLL):
            step(UNROLL * i + k, k % 2)
        return carry

    n_loop = (NCH - 1) // UNROLL
    lax.fori_loop(0, n_loop, body, 0)
    for c in range(n_loop * UNROLL, NCH):
        step(c, c % 2, last=(c == NCH - 1))

    h2 = acc_ref[...]
    gate = _sigmoid(jnp.dot(_rms(h2, gple_ref[...]).astype(BF16), wpg_ref[...], preferred_element_type=F32))
    emb = jnp.dot(p_ref[...].astype(BF16), wpp_ref[...], preferred_element_type=F32)
    h3 = h2 + emb * gate
    out_ref[...] = _rms(h3, gfin_ref[...])


def _mixer(x, mh, ah, p, weights):
    B, S, D = x.shape
    tm = TM_FF
    tok = lambda w: pl.BlockSpec((None, tm, w), lambda b, i: (b, i, 0))
    return pl.pallas_call(
        _mixer_kernel,
        grid=(B, S // tm),
        in_specs=[tok(D), tok(M_WIDTH), tok(A_WIDTH), tok(D_PLE)] + [_const_spec(w.shape) for w in weights],
        out_specs=tok(D),
        out_shape=jax.ShapeDtypeStruct((B, S, D), F32),
        scratch_shapes=[pltpu.VMEM((tm, D), BF16),
                        pltpu.VMEM((tm, D), F32),
                        pltpu.VMEM((2, tm + SUBLANES, FC), F32),
                        pltpu.VMEM((2, tm + SUBLANES, FC), F32),
                        pltpu.VMEM((NCH, 2, SUBLANES, FC), F32)],
        compiler_params=pltpu.CompilerParams(
            dimension_semantics=("arbitrary", "arbitrary"), vmem_limit_bytes=VMEM_LIMIT),
        name="mixer",
    )(x, mh, ah, p, *weights)


def _layer(h, p, g_mix, w_in, b_igate, b_fgate, w_qk_conv, b_qk_conv, g_mhead, rel_bias, w_out,
           g_ffn, w_ffn_up, w_ffn_conv, b_ffn_conv, w_ffn_down, g_ple, w_ple_gate, w_ple_proj, g_out):
    B, S, D = h.shape
    row = lambda v: v.reshape(1, -1).astype(F32)
    m_v = 2 * M_WIDTH
    m_i = 4 * M_WIDTH
    a_q = m_i + 2 * M_HEADS
    wqk = w_in[:, :m_v].astype(BF16)
    wvot = w_in[:, m_v:m_i].T.astype(BF16)
    wgate = w_in[:, m_i:a_q]
    wg = jnp.pad(wgate, ((0, 0), (0, LANES - 2 * M_HEADS))).astype(BF16)
    wgt = wgate.T.astype(BF16)
    watt = w_in[:, a_q:].astype(BF16)
    gbias = jnp.concatenate([b_igate, b_fgate]).astype(F32)
    gb = jnp.pad(gbias, (0, LANES - 2 * M_HEADS)).reshape(1, LANES)
    gbt = gbias.reshape(2 * M_HEADS, 1)

    mq, mk, mvt, mot, gcol, grow, aq, ak, av = _inproj(
        h, row(g_mix), wqk, wvot, wg, wgt, watt, w_qk_conv.astype(F32), row(b_qk_conv), gb, gbt)

    g_mhead_rep = jnp.broadcast_to(g_mhead.astype(F32)[:, None], (M_WIDTH, LANES))
    mh = _mlstm(mq, mk, mvt, mot, gcol, grow, g_mhead_rep)
    ah = _attention(aq, ak, av, rel_bias)

    weights = (
        w_out[:M_WIDTH].astype(BF16), w_out[M_WIDTH:].astype(BF16), row(g_ffn),
        w_ffn_up.astype(BF16), w_ffn_conv.astype(F32), row(b_ffn_conv),
        w_ffn_down.astype(BF16).reshape(NCH, FC, D),
        row(g_ple), w_ple_gate.astype(BF16), w_ple_proj.astype(BF16), g_out,
    )
    return _mixer(h, mh, ah, p, weights)


def kernel(x, p, g_mix, w_in, b_igate, b_fgate, w_qk_conv, b_qk_conv, g_mhead, rel_bias, w_out, g_ffn,
           w_ffn_up, w_ffn_conv, b_ffn_conv, w_ffn_down, g_ple, w_ple_gate, w_ple_proj, g_final):
    depth = w_in.shape[0]
    assert depth == 1, "the final norm is fused into the last layer's channel-mixer kernel"
    i = 0
    return _layer(x, p[i], g_mix[i], w_in[i], b_igate[i], b_fgate[i], w_qk_conv[i], b_qk_conv[i],
                  g_mhead[i], rel_bias[i], w_out[i], g_ffn[i], w_ffn_up[i], w_ffn_conv[i], b_ffn_conv[i],
                  w_ffn_down[i], g_ple[i], w_ple_gate[i], w_ple_proj[i], g_final.reshape(1, -1).astype(F32))
```

```python
import functools

import numpy as np
import jax
import jax.numpy as jnp
from jax import lax
from jax.experimental import pallas as pl
from jax.experimental.pallas import tpu as pltpu

F32 = jnp.float32
BF16 = jnp.bfloat16

D_MODEL = 1024
CHUNK = 64
M_HEADS = 4
M_WIDTH = 512
M_HEAD_DIM = 128
QK_CONV = 4
A_HEADS = 8
A_WIDTH = 512
A_HEAD_DIM = 64
BAND_CHUNKS = 8
MAX_REL = 128
D_FF = 2816
FFN_CONV = 3
D_PLE = 256
EPS = 1e-6

LANES = 128
SUBLANES = 8
NEG = -1e30
LOG2E = 1.4426950408889634

TM_IN = 1024
ML_CHUNK = 128
ML_BLOCK = 2048
TQ = 256
NKB = 1 + (BAND_CHUNKS * CHUNK) // TQ
HG = 2 * LANES // A_HEAD_DIM
Q_SUB = 4
BQ = Q_SUB * TQ
NKBB = 1 + -(-(NKB - 1) * TQ // BQ)
TM_FF = 512
FC = 256
NCH = D_FF // FC
UNROLL = 2

VMEM_LIMIT = 56 * 1024 * 1024

_NT = (((1,), (1,)), ((), ()))


def _rms(x, g):
    return x * lax.rsqrt(jnp.mean(x * x, axis=-1, keepdims=True) + EPS) * g


def _sigmoid(x):
    return 1.0 / (1.0 + jnp.exp(-x))


def _log_sigmoid(x):
    return jnp.minimum(x, 0.0) - jnp.log1p(jnp.exp(-jnp.abs(x)))


def _split3(x):
    hi = x.astype(BF16)
    r = x - hi.astype(F32)
    mid = r.astype(BF16)
    lo = (r - mid.astype(F32)).astype(BF16)
    return hi, mid, lo


def _const_spec(shape):
    n = len(shape)
    return pl.BlockSpec(shape, lambda *_: (0,) * n, pipeline_mode=pl.Buffered(1))


def _inproj_kernel(x_ref, g_ref, wqk_ref, wvot_ref, wg_ref, wgt_ref, watt_ref, cw_ref, cb_ref,
                   gb_ref, gbt_ref,
                   mq_ref, mk_ref, mvt_ref, mot_ref, gcol_ref, grow_ref, aq_ref, ak_ref, av_ref,
                   zext_ref):
    tm = x_ref.shape[0]

    @pl.when(pl.program_id(1) == 0)
    def _():
        zext_ref[0:SUBLANES, :] = jnp.zeros((SUBLANES, 2 * M_WIDTH), F32)

    a = _rms(x_ref[...], g_ref[...]).astype(BF16)

    zqk = jnp.dot(a, wqk_ref[...], preferred_element_type=F32)
    zext_ref[SUBLANES:, :] = zqk
    cw = cw_ref[...]
    acc = cb_ref[...] + cw[QK_CONV - 1:QK_CONV, :] * zqk
    for k in range(1, QK_CONV):
        acc = acc + cw[QK_CONV - 1 - k:QK_CONV - k, :] * zext_ref[SUBLANES - k:SUBLANES - k + tm, :]
    zext_ref[0:SUBLANES, :] = zext_ref[tm:tm + SUBLANES, :]
    qk = acc * _sigmoid(acc)
    mq16 = qk[:, :M_WIDTH].astype(BF16)
    mk16 = (qk[:, M_WIDTH:] * (M_HEAD_DIM ** -0.5)).astype(BF16)
    mq_ref[...] = mq16
    mk_ref[...] = mk16

    zvot = lax.dot_general(wvot_ref[...], a, _NT, preferred_element_type=F32)
    mvt_ref[...] = zvot[:M_WIDTH].astype(BF16)
    mot_ref[...] = zvot[M_WIDTH:]

    zatt = jnp.dot(a, watt_ref[...], preferred_element_type=F32)
    aq_ref[...] = (zatt[:, :A_WIDTH] * (LOG2E * A_HEAD_DIM ** -0.5)).astype(BF16)
    ak_ref[...] = zatt[:, A_WIDTH:2 * A_WIDTH].astype(BF16)
    av_ref[...] = zatt[:, 2 * A_WIDTH:].astype(BF16)

    zg = jnp.dot(a, wg_ref[...], preferred_element_type=F32) + gb_ref[...]
    lane = lax.broadcasted_iota(jnp.int32, zg.shape, 1)
    anchor = jnp.dot(jnp.concatenate([mq16, mk16], axis=1), wg_ref[...], preferred_element_type=F32)
    zero = pltpu.bitcast((pltpu.bitcast(anchor, jnp.uint32) >> 16) >> 16, F32)
    gcol_ref[...] = jnp.where(lane < M_HEADS, zg, _log_sigmoid(zg)) + zero
    zgt = lax.dot_general(wgt_ref[...], a, _NT, preferred_element_type=F32) + gbt_ref[...]
    row = lax.broadcasted_iota(jnp.int32, zgt.shape, 0)
    grow_ref[...] = jnp.where(row < M_HEADS, zgt, _log_sigmoid(zgt))


def _inproj(x, g_mix, wqk, wvot, wg, wgt, watt, cw, cb, gb, gbt):
    B, S, D = x.shape
    tm = TM_IN
    tok = lambda w: pl.BlockSpec((None, tm, w), lambda b, i: (b, i, 0))
    feat = lambda w: pl.BlockSpec((None, w, tm), lambda b, i: (b, 0, i))
    out_shape = (
        jax.ShapeDtypeStruct((B, S, M_WIDTH), BF16),
        jax.ShapeDtypeStruct((B, S, M_WIDTH), BF16),
        jax.ShapeDtypeStruct((B, M_WIDTH, S), BF16),
        jax.ShapeDtypeStruct((B, M_WIDTH, S), F32),
        jax.ShapeDtypeStruct((B, S, LANES), F32),
        jax.ShapeDtypeStruct((B, SUBLANES, S), F32),
        jax.ShapeDtypeStruct((B, S, A_WIDTH), BF16),
        jax.ShapeDtypeStruct((B, S, A_WIDTH), BF16),
        jax.ShapeDtypeStruct((B, S, A_WIDTH), BF16),
    )
    out_specs = (tok(M_WIDTH), tok(M_WIDTH), feat(M_WIDTH), feat(M_WIDTH), tok(LANES), feat(SUBLANES),
                 tok(A_WIDTH), tok(A_WIDTH), tok(A_WIDTH))
    consts = (g_mix, wqk, wvot, wg, wgt, watt, cw, cb, gb, gbt)
    return pl.pallas_call(
        _inproj_kernel,
        grid=(B, S // tm),
        in_specs=[tok(D)] + [_const_spec(w.shape) for w in consts],
        out_specs=out_specs,
        out_shape=out_shape,
        scratch_shapes=[pltpu.VMEM((tm + SUBLANES, 2 * M_WIDTH), F32)],
        compiler_params=pltpu.CompilerParams(
            dimension_semantics=("arbitrary", "arbitrary"), vmem_limit_bytes=VMEM_LIMIT),
        name="inproj",
    )(x, *consts)


def _mlstm_consts():
    L = ML_CHUNK
    s = np.arange(L)
    lower = (s[:, None] >= s[None, :]).astype(np.float32)
    upper = lower.T
    sel = np.zeros((LANES, M_HEADS * LANES), np.float32)
    for h in range(M_HEADS):
        sel[h, h * LANES:(h + 1) * LANES] = 1.0
        sel[M_HEADS + h, h * LANES:(h + 1) * LANES] = -1.0
    to_bf16 = lambda m: jnp.asarray(m, BF16)
    return (to_bf16(np.concatenate([lower] * 3, axis=1)),
            to_bf16(np.concatenate([upper] * 3, axis=0)),
            to_bf16(np.concatenate([sel] * 3, axis=0)))


def _mlstm_kernel(q_ref, k_ref, vt_ref, ot_ref, gcol_ref, grow_ref, lower3_ref, upper3_ref, sel3_ref, gh_ref,
                  out_ref, c_ref, n_ref, m_ref):
    L = ML_CHUNK
    dh = M_HEAD_DIM

    @pl.when(pl.program_id(1) == 0)
    def _():
        c_ref[...] = jnp.zeros(c_ref.shape, F32)
        n_ref[...] = jnp.zeros(n_ref.shape, F32)
        m_ref[...] = jnp.zeros(m_ref.shape, F32)

    rr = lax.broadcasted_iota(jnp.int32, (L, L), 0)
    cc = lax.broadcasted_iota(jnp.int32, (L, L), 1)
    causal_t = rr <= cc
    lane = lax.broadcasted_iota(jnp.int32, (L, LANES), 1)
    PR = 2 * SUBLANES

    def blockdiag(x, y):
        zx = jnp.zeros((x.shape[0], y.shape[1]), x.dtype)
        zy = jnp.zeros((y.shape[0], x.shape[1]), y.dtype)
        return jnp.concatenate([jnp.concatenate([x, zx], axis=1), jnp.concatenate([zy, y], axis=1)], axis=0)

    for c in range(q_ref.shape[0] // L):
        rows = slice(c * L, (c + 1) * L)
        gc = gcol_ref[rows, :]
        gr = grow_ref[:, rows]
        bc = jnp.dot(lower3_ref[...], jnp.concatenate(_split3(gc), axis=0), preferred_element_type=F32)
        br = jnp.dot(jnp.concatenate(_split3(gr), axis=1), upper3_ref[...], preferred_element_type=F32)
        ib = jnp.where(lane < M_HEADS, gc, bc)
        a_all = jnp.dot(jnp.concatenate(_split3(ib), axis=1), sel3_ref[...], preferred_element_type=F32)
        for g in range(M_HEADS // 2):
            heads = (2 * g, 2 * g + 1)
            ps = slice(2 * g * dh, (2 * g + 2) * dh)
            qp = q_ref[rows, ps]
            kp = k_ref[rows, ps]
            vtp = vt_ref[ps, rows]
            c_prev = [c_ref[h] for h in heads]
            n_prev = [n_ref[h] for h in heads]
            m_prev = [m_ref[h][0:1, 0:1] for h in heads]

            st = lax.dot_general(kp, blockdiag(qp[:, :dh], qp[:, dh:]), _NT, preferred_element_type=F32)
            n2 = [jnp.concatenate([n, n], axis=0).astype(BF16) for n in n_prev]
            cq = lax.dot_general(
                jnp.concatenate([blockdiag(c_prev[0].astype(BF16), c_prev[1].astype(BF16)),
                                 blockdiag(n2[0], n2[1])], axis=0),
                qp, _NT, preferred_element_type=F32)

            rhs, keep = [], []
            for e, h in enumerate(heads):
                a_c = a_all[:, h * dh:(h + 1) * dh]
                i_r = gr[h:h + 1, :]
                b_r = br[M_HEADS + h:M_HEADS + h + 1, :]
                b_last = b_r[:, L - 1:L]
                dmat = jnp.where(causal_t, a_c + b_r, -jnp.inf)
                inter = b_r + m_prev[e]
                m_t = jnp.maximum(inter, jnp.max(dmat, axis=0, keepdims=True))
                sw = st[:, e * L:(e + 1) * L] * jnp.exp(dmat - m_t)
                g_r = b_last - b_r + i_r
                m_new = jnp.maximum(b_last + m_prev[e], jnp.max(g_r, axis=-1, keepdims=True))
                w_c = jnp.exp(a_c + (b_last - m_new))
                kw = kp[:, e * dh:(e + 1) * dh].astype(F32) * w_c
                rhs.append(jnp.concatenate([sw.astype(BF16), kw.astype(BF16)], axis=1))
                keep.append((m_t, jnp.exp(inter - m_t), jnp.exp(b_last + m_prev[e] - m_new), m_new))

            ones = jnp.ones((PR, L), BF16)
            vs = jnp.dot(
                jnp.concatenate([blockdiag(vtp[:dh], vtp[dh:]), blockdiag(ones, ones)], axis=0),
                jnp.concatenate(rhs, axis=0), preferred_element_type=F32)

            for e, h in enumerate(heads):
                hs = slice(h * dh, (h + 1) * dh)
                m_t, w_inter, decay, m_new = keep[e]
                vr = slice(e * dh, (e + 1) * dh)
                nr = 2 * dh + e * PR
                num = vs[vr, :L] + w_inter * cq[vr, :]
                den = vs[nr:nr + 1, :L] + w_inter * cq[nr:nr + 1, :]
                ht = num * (1.0 / jnp.maximum(jnp.abs(den), jnp.exp(-m_t)))

                mu = jnp.mean(ht, axis=0, keepdims=True)
                dc = ht - mu
                var = jnp.mean(dc * dc, axis=0, keepdims=True)
                yt = dc * lax.rsqrt(var + EPS) * gh_ref[hs, :] * _sigmoid(ot_ref[hs, rows])
                out_ref[rows, hs] = yt.T.astype(out_ref.dtype)

                c_ref[h] = decay * c_prev[e] + vs[vr, L:]
                n_ref[h] = decay * n_prev[e] + vs[nr:nr + SUBLANES, L:]
                m_ref[h] = jnp.broadcast_to(m_new, (SUBLANES, LANES))


def _mlstm(mq, mk, mvt, mot, gcol, grow, g_mhead_rep):
    B, S, W = mq.shape
    tb = ML_BLOCK
    tok = lambda w: pl.BlockSpec((None, tb, w), lambda b, j: (b, j, 0))
    feat = lambda w: pl.BlockSpec((None, w, tb), lambda b, j: (b, 0, j))
    consts = _mlstm_consts() + (g_mhead_rep,)
    return pl.pallas_call(
        _mlstm_kernel,
        grid=(B, S // tb),
        in_specs=[tok(W), tok(W), feat(W), feat(W), tok(LANES), feat(SUBLANES)]
        + [_const_spec(w.shape) for w in consts],
        out_specs=tok(W),
        out_shape=jax.ShapeDtypeStruct((B, S, W), BF16),
        scratch_shapes=[pltpu.VMEM((M_HEADS, M_HEAD_DIM, M_HEAD_DIM), F32),
                        pltpu.VMEM((M_HEADS, SUBLANES, M_HEAD_DIM), F32),
                        pltpu.VMEM((M_HEADS, SUBLANES, LANES), F32)],
        compiler_params=pltpu.CompilerParams(
            dimension_semantics=("arbitrary", "arbitrary"), vmem_limit_bytes=VMEM_LIMIT),
        name="mlstm",
    )(mq, mk, mvt, mot, gcol, grow, *consts)


def _tile_delta(qi, ki):
    return qi - ki + (NKB - 1) * TQ // LANES


def _partial_deltas():
    T = LANES
    deltas = sorted({_tile_delta(qi, ki) for qi in range(TQ // T) for ki in range(NKB * TQ // T)})
    return [d for d in deltas if T * d - (T - 1) < MAX_REL and T * d + (T - 1) > -MAX_REL]


def _bias_rows(rel_bias):
    T = LANES
    x = np.arange(2 * T)
    xs = np.where(x < T, x, x - 2 * T)
    rows = []
    for d in _partial_deltas():
        idx = np.clip(T * d - xs, -MAX_REL, MAX_REL) + MAX_REL
        rows.append(rel_bias[:, idx])
    return jnp.stack(rows, axis=1).astype(F32)


def _build_band_bias(rb_ref, rows_ref, bias_ref):
    T = LANES
    rr = lax.broadcasted_iota(jnp.int32, (T, T), 0)
    cc = lax.broadcasted_iota(jnp.int32, (T, T), 1)
    rr2 = lax.broadcasted_iota(jnp.int32, (T, 2 * T), 0)
    back = (NKB - 1) * TQ
    partial = _partial_deltas()
    for h in range(A_HEADS):
        toeplitz = {}
        for n, delta in enumerate(partial):
            tab = jnp.broadcast_to(rows_ref[h, n:n + 1, :], (T, 2 * T))
            for bit in range(T.bit_length() - 1):
                rolled = pltpu.roll(tab, 1 << bit, 1)
                tab = jnp.where(((rr2 >> bit) & 1) == 1, rolled, tab)
            toeplitz[delta] = tab[:, :T]

        for qi in range(TQ // T):
            for ki in range(NKB * TQ // T):
                qc = (rr + qi * T + back) // CHUNK
                kc = (cc + ki * T) // CHUNK
                visible = (kc <= qc) & (kc >= qc - BAND_CHUNKS)
                q_lo, q_hi = (qi * T + back) // CHUNK, (qi * T + T - 1 + back) // CHUNK
                k_lo, k_hi = (ki * T) // CHUNK, (ki * T + T - 1) // CHUNK
                delta = _tile_delta(qi, ki)
                if k_lo > q_hi or k_hi < q_lo - BAND_CHUNKS:
                    tile = jnp.full((T, T), NEG, F32)
                else:
                    if delta in toeplitz:
                        vals = toeplitz[delta]
                    else:
                        far = 2 * MAX_REL if delta > 0 else 0
                        vals = jnp.full((T, T), rb_ref[h, far], F32)
                    tile = jnp.where(visible, vals * LOG2E, NEG)
                bias_ref[h, qi * T:(qi + 1) * T, ki * T:(ki + 1) * T] = tile


def _attn_kernel(rb_ref, rows_ref, q_ref, *refs):
    k_refs = refs[:NKBB]
    v_refs = refs[NKBB:2 * NKBB]
    out_ref = refs[2 * NKBB]
    bias_ref = refs[2 * NKBB + 1]
    j = pl.program_id(1)
    nk = NKB * TQ
    back = (NKB - 1) * TQ
    first = (NKBB - 1) * BQ - back

    def window(blocks, u, gs):
        lo, hi = first + u * TQ, first + u * TQ + nk
        pieces = []
        for n, r in enumerate(blocks):
            a, b = max(lo, n * BQ), min(hi, (n + 1) * BQ)
            if a < b:
                pieces.append(r[a - n * BQ:b - n * BQ, gs])
        return jnp.concatenate(pieces, axis=0)

    @pl.when((pl.program_id(0) == 0) & (j == 0))
    def _():
        _build_band_bias(rb_ref, rows_ref, bias_ref)

    def heads(at_start):
        for u in range(Q_SUB):
            query_block(u, at_start)

    def query_block(u, at_start):
        qs = slice(u * TQ, (u + 1) * TQ)
        if at_start:
            kpos = lax.broadcasted_iota(jnp.int32, (1, nk), 1)
            valid = kpos >= back - u * TQ - j * BQ
        gw = HG * A_HEAD_DIM
        lane_head = lax.broadcasted_iota(jnp.int32, (1, gw), 1) // A_HEAD_DIM
        onehot = [(lane_head == e).astype(BF16) for e in range(HG)]
        for g in range(A_HEADS // HG):
            gs = slice(g * gw, (g + 1) * gw)
            kq = window(k_refs, u, gs)
            vq = window(v_refs, u, gs)
            kbd = jnp.concatenate([kq * onehot[e] for e in range(HG)], axis=0)
            vbd = jnp.concatenate([vq * onehot[e] for e in range(HG)], axis=0)
            s = lax.dot_general(q_ref[qs, gs], kbd, _NT, preferred_element_type=F32)
            probs, scale = [], None
            for e in range(HG):
                sh = s[:, e * nk:(e + 1) * nk] + bias_ref[g * HG + e]
                if at_start:
                    sh = jnp.where(valid, sh, NEG)
                m = jnp.max(sh, axis=-1, keepdims=True)
                p = jnp.exp2(sh - m)
                rl = 1.0 / jnp.sum(p, axis=-1, keepdims=True)
                probs.append(p.astype(BF16))
                scale = rl if e == 0 else jnp.where(lane_head >= e, rl, scale)
            o = jnp.dot(jnp.concatenate(probs, axis=1), vbd, preferred_element_type=F32) * scale
            out_ref[qs, gs] = o.astype(out_ref.dtype)

    pl.when(j * BQ < back)(functools.partial(heads, True))
    pl.when(j * BQ >= back)(functools.partial(heads, False))


def _attention(aq, ak, av, rel_bias):
    B, S, W = aq.shape
    blk = lambda back: pl.BlockSpec((None, BQ, W), lambda b, j: (b, jnp.maximum(j - back, 0), 0))
    kv_specs = [blk(NKBB - 1 - n) for n in range(NKBB)]
    rows = _bias_rows(rel_bias)
    return pl.pallas_call(
        _attn_kernel,
        grid=(B, S // BQ),
        in_specs=[pl.BlockSpec(memory_space=pltpu.SMEM), _const_spec(rows.shape), blk(0)] + kv_specs + kv_specs,
        out_specs=blk(0),
        out_shape=jax.ShapeDtypeStruct((B, S, W), BF16),
        scratch_shapes=[pltpu.VMEM((A_HEADS, TQ, NKB * TQ), F32)],
        compiler_params=pltpu.CompilerParams(
            dimension_semantics=("arbitrary", "arbitrary"), vmem_limit_bytes=VMEM_LIMIT),
        name="band_attn",
    )(rel_bias.astype(F32), rows, aq, *([ak] * NKBB), *([av] * NKBB))


def _gelu_tanh(x):
    c = 0.7978845608028654
    half = 0.5 * x
    return half * jnp.tanh(x * (c + (c * 0.044715) * (x * x))) + half


def _mixer_kernel(x_ref, mh_ref, ah_ref, p_ref, woa_ref, wob_ref, gffn_ref, wu_ref, cw_ref, cb_ref,
                  wd_ref, gple_ref, wpg_ref, wpp_ref, gfin_ref,
                  out_ref, a_ref, acc_ref, u0_ref, u1_ref, carry_ref):
    tm = x_ref.shape[0]

    @pl.when(pl.program_id(1) == 0)
    def _():
        carry_ref[...] = jnp.zeros(carry_ref.shape, F32)

    h1 = (x_ref[...]
          + jnp.dot(mh_ref[...], woa_ref[...], preferred_element_type=F32)
          + jnp.dot(ah_ref[...], wob_ref[...], preferred_element_type=F32))
    acc_ref[...] = h1
    a_ref[...] = _rms(h1, gffn_ref[...]).astype(BF16)

    def cols(n):
        if isinstance(n, int):
            return slice(n * FC, (n + 1) * FC)
        return pl.ds(pl.multiple_of(n * FC, FC), FC)

    def up(c, u_ref):
        a = a_ref[...]
        u_ref[0, 0:SUBLANES, :] = carry_ref[c, 0]
        u_ref[1, 0:SUBLANES, :] = carry_ref[c, 1]
        u_ref[0, SUBLANES:, :] = jnp.dot(a, wu_ref[:, cols(c)], preferred_element_type=F32)
        u_ref[1, SUBLANES:, :] = jnp.dot(a, wu_ref[:, cols(NCH + c)], preferred_element_type=F32)

    def conv(u_ref, part, w, b):
        y = b + w[FFN_CONV - 1:FFN_CONV, :] * u_ref[part, SUBLANES:, :]
        for k in range(1, FFN_CONV):
            y = y + w[FFN_CONV - 1 - k:FFN_CONV - k, :] * u_ref[part, SUBLANES - k:SUBLANES - k + tm, :]
        return y

    def down(c, u_ref):
        gate = conv(u_ref, 0, cw_ref[:, cols(c)], cb_ref[:, cols(c)])
        val = conv(u_ref, 1, cw_ref[:, cols(NCH + c)], cb_ref[:, cols(NCH + c)])
        carry_ref[c, 0] = u_ref[0, tm:tm + SUBLANES, :]
        carry_ref[c, 1] = u_ref[1, tm:tm + SUBLANES, :]
        act = (_gelu_tanh(gate) * val).astype(BF16)
        acc_ref[...] += jnp.dot(act, wd_ref[c], preferred_element_type=F32)

    u_refs = (u0_ref, u1_ref)

    def step(c, par, last=False):
        if not last:
            up(c + 1, u_refs[1 - par])
        down(c, u_refs[par])

    assert UNROLL % 2 == 0
    up(0, u0_ref)

    def body(i, carry):
        for k in range(UNROLL):
            step(UNROLL * i + k, k % 2)
        return carry

    n_loop = (NCH - 1) // UNROLL
    lax.fori_loop(0, n_loop, body, 0)
    for c in range(n_loop * UNROLL, NCH):
        step(c, c % 2, last=(c == NCH - 1))

    h2 = acc_ref[...]
    gate = _sigmoid(jnp.dot(_rms(h2, gple_ref[...]).astype(BF16), wpg_ref[...], preferred_element_type=F32))
    emb = jnp.dot(p_ref[...].astype(BF16), wpp_ref[...], preferred_element_type=F32)
    h3 = h2 + emb * gate
    out_ref[...] = _rms(h3, gfin_ref[...])


def _mixer(x, mh, ah, p, weights):
    B, S, D = x.shape
    tm = TM_FF
    tok = lambda w: pl.BlockSpec((None, tm, w), lambda b, i: (b, i, 0))
    return pl.pallas_call(
        _mixer_kernel,
        grid=(B, S // tm),
        in_specs=[tok(D), tok(M_WIDTH), tok(A_WIDTH), tok(D_PLE)] + [_const_spec(w.shape) for w in weights],
        out_specs=tok(D),
        out_shape=jax.ShapeDtypeStruct((B, S, D), F32),
        scratch_shapes=[pltpu.VMEM((tm, D), BF16),
                        pltpu.VMEM((tm, D), F32),
                        pltpu.VMEM((2, tm + SUBLANES, FC), F32),
                        pltpu.VMEM((2, tm + SUBLANES, FC), F32),
                        pltpu.VMEM((NCH, 2, SUBLANES, FC), F32)],
        compiler_params=pltpu.CompilerParams(
            dimension_semantics=("arbitrary", "arbitrary"), vmem_limit_bytes=VMEM_LIMIT),
        name="mixer",
    )(x, mh, ah, p, *weights)


def _layer(h, p, g_mix, w_in, b_igate, b_fgate, w_qk_conv, b_qk_conv, g_mhead, rel_bias, w_out,
           g_ffn, w_ffn_up, w_ffn_conv, b_ffn_conv, w_ffn_down, g_ple, w_ple_gate, w_ple_proj, g_out):
    B, S, D = h.shape
    row = lambda v: v.reshape(1, -1).astype(F32)
    m_v = 2 * M_WIDTH
    m_i = 4 * M_WIDTH
    a_q = m_i + 2 * M_HEADS
    wqk = w_in[:, :m_v].astype(BF16)
    wvot = w_in[:, m_v:m_i].T.astype(BF16)
    wgate = w_in[:, m_i:a_q]
    wg = jnp.pad(wgate, ((0, 0), (0, LANES - 2 * M_HEADS))).astype(BF16)
    wgt = wgate.T.astype(BF16)
    watt = w_in[:, a_q:].astype(BF16)
    gbias = jnp.concatenate([b_igate, b_fgate]).astype(F32)
    gb = jnp.pad(gbias, (0, LANES - 2 * M_HEADS)).reshape(1, LANES)
    gbt = gbias.reshape(2 * M_HEADS, 1)

    mq, mk, mvt, mot, gcol, grow, aq, ak, av = _inproj(
        h, row(g_mix), wqk, wvot, wg, wgt, watt, w_qk_conv.astype(F32), row(b_qk_conv), gb, gbt)

    g_mhead_rep = jnp.broadcast_to(g_mhead.astype(F32)[:, None], (M_WIDTH, LANES))
    mh = _mlstm(mq, mk, mvt, mot, gcol, grow, g_mhead_rep)
    ah = _attention(aq, ak, av, rel_bias)

    weights = (
        w_out[:M_WIDTH].astype(BF16), w_out[M_WIDTH:].astype(BF16), row(g_ffn),
        w_ffn_up.astype(BF16), w_ffn_conv.astype(F32), row(b_ffn_conv),
        w_ffn_down.astype(BF16).reshape(NCH, FC, D),
        row(g_ple), w_ple_gate.astype(BF16), w_ple_proj.astype(BF16), g_out,
    )
    return _mixer(h, mh, ah, p, weights)


def kernel(x, p, g_mix, w_in, b_igate, b_fgate, w_qk_conv, b_qk_conv, g_mhead, rel_bias, w_out, g_ffn,
           w_ffn_up, w_ffn_conv, b_ffn_conv, w_ffn_down, g_ple, w_ple_gate, w_ple_proj, g_final):
    depth = w_in.shape[0]
    assert depth == 1, "the final norm is fused into the last layer's channel-mixer kernel"
    i = 0
    return _layer(x, p[i], g_mix[i], w_in[i], b_igate[i], b_fgate[i], w_qk_conv[i], b_qk_conv[i],
                  g_mhead[i], rel_bias[i], w_out[i], g_ffn[i], w_ffn_up[i], w_ffn_conv[i], b_ffn_conv[i],
                  w_ffn_down[i], g_ple[i], w_ple_gate[i], w_ple_proj[i], g_final.reshape(1, -1).astype(F32))
```

```python
import functools

import numpy as np
import jax
import jax.numpy as jnp
from jax import lax
from jax.experimental import pallas as pl
from jax.experimental.pallas import tpu as pltpu

F32 = jnp.float32
BF16 = jnp.bfloat16

D_MODEL = 1024
CHUNK = 64
M_HEADS = 4
M_WIDTH = 512
M_HEAD_DIM = 128
QK_CONV = 4
A_HEADS = 8
A_WIDTH = 512
A_HEAD_DIM = 64
BAND_CHUNKS = 8
MAX_REL = 128
D_FF = 2816
FFN_CONV = 3
D_PLE = 256
EPS = 1e-6

LANES = 128
SUBLANES = 8
NEG = -1e30
LOG2E = 1.4426950408889634

TM_IN = 1024
ML_CHUNK = 128
ML_BLOCK = 2048
TQ = 256
NKB = 1 + (BAND_CHUNKS * CHUNK) // TQ
HG = 2 * LANES // A_HEAD_DIM
Q_SUB = 4
BQ = Q_SUB * TQ
NKBB = 1 + -(-(NKB - 1) * TQ // BQ)
TM_FF = 512
FC = 256
NCH = D_FF // FC
UNROLL = 2

VMEM_LIMIT = 56 * 1024 * 1024

_NT = (((1,), (1,)), ((), ()))


def _rms(x, g):
    return x * lax.rsqrt(jnp.mean(x * x, axis=-1, keepdims=True) + EPS) * g


def _sigmoid(x):
    return 1.0 / (1.0 + jnp.exp(-x))


def _log_sigmoid(x):
    return jnp.minimum(x, 0.0) - jnp.log1p(jnp.exp(-jnp.abs(x)))


def _split3(x):
    hi = x.astype(BF16)
    r = x - hi.astype(F32)
    mid = r.astype(BF16)
    lo = (r - mid.astype(F32)).astype(BF16)
    return hi, mid, lo


def _const_spec(shape):
    n = len(shape)
    return pl.BlockSpec(shape, lambda *_: (0,) * n, pipeline_mode=pl.Buffered(1))


def _inproj_kernel(x_ref, g_ref, wqk_ref, wvot_ref, wg_ref, wgt_ref, watt_ref, cw_ref, cb_ref,
                   gb_ref, gbt_ref,
                   mq_ref, mk_ref, mvt_ref, mot_ref, gcol_ref, grow_ref, aq_ref, ak_ref, av_ref,
                   zext_ref):
    tm = x_ref.shape[0]

    @pl.when(pl.program_id(1) == 0)
    def _():
        zext_ref[0:SUBLANES, :] = jnp.zeros((SUBLANES, 2 * M_WIDTH), F32)

    a = _rms(x_ref[...], g_ref[...]).astype(BF16)

    zqk = jnp.dot(a, wqk_ref[...], preferred_element_type=F32)
    zext_ref[SUBLANES:, :] = zqk
    cw = cw_ref[...]
    acc = cb_ref[...] + cw[QK_CONV - 1:QK_CONV, :] * zqk
    for k in range(1, QK_CONV):
        acc = acc + cw[QK_CONV - 1 - k:QK_CONV - k, :] * zext_ref[SUBLANES - k:SUBLANES - k + tm, :]
    zext_ref[0:SUBLANES, :] = zext_ref[tm:tm + SUBLANES, :]
    qk = acc * _sigmoid(acc)
    mq16 = qk[:, :M_WIDTH].astype(BF16)
    mk16 = (qk[:, M_WIDTH:] * (M_HEAD_DIM ** -0.5)).astype(BF16)
    mq_ref[...] = mq16
    mk_ref[...] = mk16

    zvot = lax.dot_general(wvot_ref[...], a, _NT, preferred_element_type=F32)
    mvt_ref[...] = zvot[:M_WIDTH].astype(BF16)
    mot_ref[...] = zvot[M_WIDTH:]

    zatt = jnp.dot(a, watt_ref[...], preferred_element_type=F32)
    aq_ref[...] = (zatt[:, :A_WIDTH] * (LOG2E * A_HEAD_DIM ** -0.5)).astype(BF16)
    ak_ref[...] = zatt[:, A_WIDTH:2 * A_WIDTH].astype(BF16)
    av_ref[...] = zatt[:, 2 * A_WIDTH:].astype(BF16)

    zg = jnp.dot(a, wg_ref[...], preferred_element_type=F32) + gb_ref[...]
    lane = lax.broadcasted_iota(jnp.int32, zg.shape, 1)
    half = M_WIDTH // 2
    folded = mq16[:, :half] + mq16[:, half:] + mk16[:, :half] + mk16[:, half:]
    anchor = jnp.dot(folded, wg_ref[0:half, :], preferred_element_type=F32)
    zero = pltpu.bitcast((pltpu.bitcast(anchor, jnp.uint32) >> 16) >> 16, F32)
    gcol_ref[...] = jnp.where(lane < M_HEADS, zg, _log_sigmoid(zg)) + zero
    zgt = lax.dot_general(wgt_ref[...], a, _NT, preferred_element_type=F32) + gbt_ref[...]
    row = lax.broadcasted_iota(jnp.int32, zgt.shape, 0)
    grow_ref[...] = jnp.where(row < M_HEADS, zgt, _log_sigmoid(zgt))


def _inproj(x, g_mix, wqk, wvot, wg, wgt, watt, cw, cb, gb, gbt):
    B, S, D = x.shape
    tm = TM_IN
    tok = lambda w: pl.BlockSpec((None, tm, w), lambda b, i: (b, i, 0))
    feat = lambda w: pl.BlockSpec((None, w, tm), lambda b, i: (b, 0, i))
    out_shape = (
        jax.ShapeDtypeStruct((B, S, M_WIDTH), BF16),
        jax.ShapeDtypeStruct((B, S, M_WIDTH), BF16),
        jax.ShapeDtypeStruct((B, M_WIDTH, S), BF16),
        jax.ShapeDtypeStruct((B, M_WIDTH, S), F32),
        jax.ShapeDtypeStruct((B, S, LANES), F32),
        jax.ShapeDtypeStruct((B, SUBLANES, S), F32),
        jax.ShapeDtypeStruct((B, S, A_WIDTH), BF16),
        jax.ShapeDtypeStruct((B, S, A_WIDTH), BF16),
        jax.ShapeDtypeStruct((B, S, A_WIDTH), BF16),
    )
    out_specs = (tok(M_WIDTH), tok(M_WIDTH), feat(M_WIDTH), feat(M_WIDTH), tok(LANES), feat(SUBLANES),
                 tok(A_WIDTH), tok(A_WIDTH), tok(A_WIDTH))
    consts = (g_mix, wqk, wvot, wg, wgt, watt, cw, cb, gb, gbt)
    return pl.pallas_call(
        _inproj_kernel,
        grid=(B, S // tm),
        in_specs=[tok(D)] + [_const_spec(w.shape) for w in consts],
        out_specs=out_specs,
        out_shape=out_shape,
        scratch_shapes=[pltpu.VMEM((tm + SUBLANES, 2 * M_WIDTH), F32)],
        compiler_params=pltpu.CompilerParams(
            dimension_semantics=("arbitrary", "arbitrary"), vmem_limit_bytes=VMEM_LIMIT),
        name="inproj",
    )(x, *consts)


def _mlstm_consts():
    L = ML_CHUNK
    s = np.arange(L)
    lower = (s[:, None] >= s[None, :]).astype(np.float32)
    upper = lower.T
    sel = np.zeros((LANES, M_HEADS * LANES), np.float32)
    for h in range(M_HEADS):
        sel[h, h * LANES:(h + 1) * LANES] = 1.0
        sel[M_HEADS + h, h * LANES:(h + 1) * LANES] = -1.0
    to_bf16 = lambda m: jnp.asarray(m, BF16)
    return (to_bf16(np.concatenate([lower] * 3, axis=1)),
            to_bf16(np.concatenate([upper] * 3, axis=0)),
            to_bf16(np.concatenate([sel] * 3, axis=0)))


def _mlstm_kernel(q_ref, k_ref, vt_ref, ot_ref, gcol_ref, grow_ref, lower3_ref, upper3_ref, sel3_ref, gh_ref,
                  out_ref, c_ref, n_ref, m_ref):
    L = ML_CHUNK
    dh = M_HEAD_DIM

    @pl.when(pl.program_id(1) == 0)
    def _():
        c_ref[...] = jnp.zeros(c_ref.shape, F32)
        n_ref[...] = jnp.zeros(n_ref.shape, F32)
        m_ref[...] = jnp.zeros(m_ref.shape, F32)

    rr = lax.broadcasted_iota(jnp.int32, (L, L), 0)
    cc = lax.broadcasted_iota(jnp.int32, (L, L), 1)
    causal_t = rr <= cc
    lane = lax.broadcasted_iota(jnp.int32, (L, LANES), 1)
    PR = 2 * SUBLANES

    def blockdiag(x, y):
        zx = jnp.zeros((x.shape[0], y.shape[1]), x.dtype)
        zy = jnp.zeros((y.shape[0], x.shape[1]), y.dtype)
        return jnp.concatenate([jnp.concatenate([x, zx], axis=1), jnp.concatenate([zy, y], axis=1)], axis=0)

    for c in range(q_ref.shape[0] // L):
        rows = slice(c * L, (c + 1) * L)
        gc = gcol_ref[rows, :]
        gr = grow_ref[:, rows]
        bc = jnp.dot(lower3_ref[...], jnp.concatenate(_split3(gc), axis=0), preferred_element_type=F32)
        br = jnp.dot(jnp.concatenate(_split3(gr), axis=1), upper3_ref[...], preferred_element_type=F32)
        ib = jnp.where(lane < M_HEADS, gc, bc)
        a_all = jnp.dot(jnp.concatenate(_split3(ib), axis=1), sel3_ref[...], preferred_element_type=F32)
        for g in range(M_HEADS // 2):
            heads = (2 * g, 2 * g + 1)
            ps = slice(2 * g * dh, (2 * g + 2) * dh)
            qp = q_ref[rows, ps]
            kp = k_ref[rows, ps]
            vtp = vt_ref[ps, rows]
            c_prev = [c_ref[h] for h in heads]
            n_prev = [n_ref[h] for h in heads]
            m_prev = [m_ref[h][0:1, 0:1] for h in heads]

            st = lax.dot_general(kp, blockdiag(qp[:, :dh], qp[:, dh:]), _NT, preferred_element_type=F32)
            n2 = [jnp.concatenate([n, n], axis=0).astype(BF16) for n in n_prev]
            cq = lax.dot_general(
                jnp.concatenate([blockdiag(c_prev[0].astype(BF16), c_prev[1].astype(BF16)),
                                 blockdiag(n2[0], n2[1])], axis=0),
                qp, _NT, preferred_element_type=F32)

            rhs, keep = [], []
            for e, h in enumerate(heads):
                a_c = a_all[:, h * dh:(h + 1) * dh]
                i_r = gr[h:h + 1, :]
                b_r = br[M_HEADS + h:M_HEADS + h + 1, :]
                b_last = b_r[:, L - 1:L]
                dmat = jnp.where(causal_t, a_c + b_r, -jnp.inf)
                inter = b_r + m_prev[e]
                m_t = jnp.maximum(inter, jnp.max(dmat, axis=0, keepdims=True))
                sw = st[:, e * L:(e + 1) * L] * jnp.exp(dmat - m_t)
                g_r = b_last - b_r + i_r
                m_new = jnp.maximum(b_last + m_prev[e], jnp.max(g_r, axis=-1, keepdims=True))
                w_c = jnp.exp(a_c + (b_last - m_new))
                kw = kp[:, e * dh:(e + 1) * dh].astype(F32) * w_c
                rhs.append(jnp.concatenate([sw.astype(BF16), kw.astype(BF16)], axis=1))
                keep.append((m_t, jnp.exp(inter - m_t), jnp.exp(b_last + m_prev[e] - m_new), m_new))

            ones = jnp.ones((PR, L), BF16)
            vs = jnp.dot(
                jnp.concatenate([blockdiag(vtp[:dh], vtp[dh:]), blockdiag(ones, ones)], axis=0),
                jnp.concatenate(rhs, axis=0), preferred_element_type=F32)

            for e, h in enumerate(heads):
                hs = slice(h * dh, (h + 1) * dh)
                m_t, w_inter, decay, m_new = keep[e]
                vr = slice(e * dh, (e + 1) * dh)
                nr = 2 * dh + e * PR
                num = vs[vr, :L] + w_inter * cq[vr, :]
                den = vs[nr:nr + 1, :L] + w_inter * cq[nr:nr + 1, :]
                ht = num * (1.0 / jnp.maximum(jnp.abs(den), jnp.exp(-m_t)))

                mu = jnp.mean(ht, axis=0, keepdims=True)
                dc = ht - mu
                var = jnp.mean(dc * dc, axis=0, keepdims=True)
                yt = dc * lax.rsqrt(var + EPS) * gh_ref[hs, :] * _sigmoid(ot_ref[hs, rows])
                out_ref[rows, hs] = yt.T.astype(out_ref.dtype)

                c_ref[h] = decay * c_prev[e] + vs[vr, L:]
                n_ref[h] = decay * n_prev[e] + vs[nr:nr + SUBLANES, L:]
                m_ref[h] = jnp.broadcast_to(m_new, (SUBLANES, LANES))


def _mlstm(mq, mk, mvt, mot, gcol, grow, g_mhead_rep):
    B, S, W = mq.shape
    tb = ML_BLOCK
    tok = lambda w: pl.BlockSpec((None, tb, w), lambda b, j: (b, j, 0))
    feat = lambda w: pl.BlockSpec((None, w, tb), lambda b, j: (b, 0, j))
    consts = _mlstm_consts() + (g_mhead_rep,)
    return pl.pallas_call(
        _mlstm_kernel,
        grid=(B, S // tb),
        in_specs=[tok(W), tok(W), feat(W), feat(W), tok(LANES), feat(SUBLANES)]
        + [_const_spec(w.shape) for w in consts],
        out_specs=tok(W),
        out_shape=jax.ShapeDtypeStruct((B, S, W), BF16),
        scratch_shapes=[pltpu.VMEM((M_HEADS, M_HEAD_DIM, M_HEAD_DIM), F32),
                        pltpu.VMEM((M_HEADS, SUBLANES, M_HEAD_DIM), F32),
                        pltpu.VMEM((M_HEADS, SUBLANES, LANES), F32)],
        compiler_params=pltpu.CompilerParams(
            dimension_semantics=("arbitrary", "arbitrary"), vmem_limit_bytes=VMEM_LIMIT),
        name="mlstm",
    )(mq, mk, mvt, mot, gcol, grow, *consts)


def _tile_delta(qi, ki):
    return qi - ki + (NKB - 1) * TQ // LANES


def _partial_deltas():
    T = LANES
    deltas = sorted({_tile_delta(qi, ki) for qi in range(TQ // T) for ki in range(NKB * TQ // T)})
    return [d for d in deltas if T * d - (T - 1) < MAX_REL and T * d + (T - 1) > -MAX_REL]


def _bias_rows(rel_bias):
    T = LANES
    x = np.arange(2 * T)
    xs = np.where(x < T, x, x - 2 * T)
    rows = []
    for d in _partial_deltas():
        idx = np.clip(T * d - xs, -MAX_REL, MAX_REL) + MAX_REL
        rows.append(rel_bias[:, idx])
    return jnp.stack(rows, axis=1).astype(F32)


def _build_band_bias(rb_ref, rows_ref, bias_ref):
    T = LANES
    rr = lax.broadcasted_iota(jnp.int32, (T, T), 0)
    cc = lax.broadcasted_iota(jnp.int32, (T, T), 1)
    rr2 = lax.broadcasted_iota(jnp.int32, (T, 2 * T), 0)
    back = (NKB - 1) * TQ
    partial = _partial_deltas()
    for h in range(A_HEADS):
        toeplitz = {}
        for n, delta in enumerate(partial):
            tab = jnp.broadcast_to(rows_ref[h, n:n + 1, :], (T, 2 * T))
            for bit in range(T.bit_length() - 1):
                rolled = pltpu.roll(tab, 1 << bit, 1)
                tab = jnp.where(((rr2 >> bit) & 1) == 1, rolled, tab)
            toeplitz[delta] = tab[:, :T]

        for qi in range(TQ // T):
            for ki in range(NKB * TQ // T):
                qc = (rr + qi * T + back) // CHUNK
                kc = (cc + ki * T) // CHUNK
                visible = (kc <= qc) & (kc >= qc - BAND_CHUNKS)
                q_lo, q_hi = (qi * T + back) // CHUNK, (qi * T + T - 1 + back) // CHUNK
                k_lo, k_hi = (ki * T) // CHUNK, (ki * T + T - 1) // CHUNK
                delta = _tile_delta(qi, ki)
                if k_lo > q_hi or k_hi < q_lo - BAND_CHUNKS:
                    tile = jnp.full((T, T), NEG, F32)
                else:
                    if delta in toeplitz:
                        vals = toeplitz[delta]
                    else:
                        far = 2 * MAX_REL if delta > 0 else 0
                        vals = jnp.full((T, T), rb_ref[h, far], F32)
                    tile = jnp.where(visible, vals * LOG2E, NEG)
                bias_ref[h, qi * T:(qi + 1) * T, ki * T:(ki + 1) * T] = tile


def _attn_kernel(rb_ref, rows_ref, q_ref, *refs):
    k_refs = refs[:NKBB]
    v_refs = refs[NKBB:2 * NKBB]
    out_ref = refs[2 * NKBB]
    bias_ref = refs[2 * NKBB + 1]
    j = pl.program_id(1)
    nk = NKB * TQ
    back = (NKB - 1) * TQ
    first = (NKBB - 1) * BQ - back

    def window(blocks, u, gs):
        lo, hi = first + u * TQ, first + u * TQ + nk
        pieces = []
        for n, r in enumerate(blocks):
            a, b = max(lo, n * BQ), min(hi, (n + 1) * BQ)
            if a < b:
                pieces.append(r[a - n * BQ:b - n * BQ, gs])
        return jnp.concatenate(pieces, axis=0)

    @pl.when((pl.program_id(0) == 0) & (j == 0))
    def _():
        _build_band_bias(rb_ref, rows_ref, bias_ref)

    def heads(at_start):
        for u in range(Q_SUB):
            query_block(u, at_start)

    def query_block(u, at_start):
        qs = slice(u * TQ, (u + 1) * TQ)
        if at_start:
            kpos = lax.broadcasted_iota(jnp.int32, (1, nk), 1)
            valid = kpos >= back - u * TQ - j * BQ
        gw = HG * A_HEAD_DIM
        lane_head = lax.broadcasted_iota(jnp.int32, (1, gw), 1) // A_HEAD_DIM
        onehot = [(lane_head == e).astype(BF16) for e in range(HG)]
        for g in range(A_HEADS // HG):
            gs = slice(g * gw, (g + 1) * gw)
            kq = window(k_refs, u, gs)
            vq = window(v_refs, u, gs)
            kbd = jnp.concatenate([kq * onehot[e] for e in range(HG)], axis=0)
            vbd = jnp.concatenate([vq * onehot[e] for e in range(HG)], axis=0)
            s = lax.dot_general(q_ref[qs, gs], kbd, _NT, preferred_element_type=F32)
            probs, scale = [], None
            for e in range(HG):
                sh = s[:, e * nk:(e + 1) * nk] + bias_ref[g * HG + e]
                if at_start:
                    sh = jnp.where(valid, sh, NEG)
                m = jnp.max(sh, axis=-1, keepdims=True)
                p = jnp.exp2(sh - m)
                rl = 1.0 / jnp.sum(p, axis=-1, keepdims=True)
                probs.append(p.astype(BF16))
                scale = rl if e == 0 else jnp.where(lane_head >= e, rl, scale)
            o = jnp.dot(jnp.concatenate(probs, axis=1), vbd, preferred_element_type=F32) * scale
            out_ref[qs, gs] = o.astype(out_ref.dtype)

    pl.when(j * BQ < back)(functools.partial(heads, True))
    pl.when(j * BQ >= back)(functools.partial(heads, False))


def _attention(aq, ak, av, rel_bias):
    B, S, W = aq.shape
    blk = lambda back: pl.BlockSpec((None, BQ, W), lambda b, j: (b, jnp.maximum(j - back, 0), 0))
    kv_specs = [blk(NKBB - 1 - n) for n in range(NKBB)]
    rows = _bias_rows(rel_bias)
    return pl.pallas_call(
        _attn_kernel,
        grid=(B, S // BQ),
        in_specs=[pl.BlockSpec(memory_space=pltpu.SMEM), _const_spec(rows.shape), blk(0)] + kv_specs + kv_specs,
        out_specs=blk(0),
        out_shape=jax.ShapeDtypeStruct((B, S, W), BF16),
        scratch_shapes=[pltpu.VMEM((A_HEADS, TQ, NKB * TQ), F32)],
        compiler_params=pltpu.CompilerParams(
            dimension_semantics=("arbitrary", "arbitrary"), vmem_limit_bytes=VMEM_LIMIT),
        name="band_attn",
    )(rel_bias.astype(F32), rows, aq, *([ak] * NKBB), *([av] * NKBB))


def _gelu_tanh(x):
    c = 0.7978845608028654
    half = 0.5 * x
    return half * jnp.tanh(x * (c + (c * 0.044715) * (x * x))) + half


def _mixer_kernel(x_ref, mh_ref, ah_ref, p_ref, woa_ref, wob_ref, gffn_ref, wu_ref, cw_ref, cb_ref,
                  wd_ref, gple_ref, wpg_ref, wpp_ref, gfin_ref,
                  out_ref, a_ref, acc_ref, u0_ref, u1_ref, carry_ref):
    tm = x_ref.shape[0]

    @pl.when(pl.program_id(1) == 0)
    def _():
        carry_ref[...] = jnp.zeros(carry_ref.shape, F32)

    h1 = (x_ref[...]
          + jnp.dot(mh_ref[...], woa_ref[...], preferred_element_type=F32)
          + jnp.dot(ah_ref[...], wob_ref[...], preferred_element_type=F32))
    acc_ref[...] = h1
    a_ref[...] = _rms(h1, gffn_ref[...]).astype(BF16)

    def cols(n):
        if isinstance(n, int):
            return slice(n * FC, (n + 1) * FC)
        return pl.ds(pl.multiple_of(n * FC, FC), FC)

    def up(c, u_ref):
        a = a_ref[...]
        u_ref[0, 0:SUBLANES, :] = carry_ref[c, 0]
        u_ref[1, 0:SUBLANES, :] = carry_ref[c, 1]
        u_ref[0, SUBLANES:, :] = jnp.dot(a, wu_ref[:, cols(c)], preferred_element_type=F32)
        u_ref[1, SUBLANES:, :] = jnp.dot(a, wu_ref[:, cols(NCH + c)], preferred_element_type=F32)

    def conv(u_ref, part, w, b):
        y = b + w[FFN_CONV - 1:FFN_CONV, :] * u_ref[part, SUBLANES:, :]
        for k in range(1, FFN_CONV):
            y = y + w[FFN_CONV - 1 - k:FFN_CONV - k, :] * u_ref[part, SUBLANES - k:SUBLANES - k + tm, :]
        return y

    def down(c, u_ref):
        gate = conv(u_ref, 0, cw_ref[:, cols(c)], cb_ref[:, cols(c)])
        val = conv(u_ref, 1, cw_ref[:, cols(NCH + c)], cb_ref[:, cols(NCH + c)])
        carry_ref[c, 0] = u_ref[0, tm:tm + SUBLANES, :]
        carry_ref[c, 1] = u_ref[1, tm:tm + SUBLANES, :]
        act = (_gelu_tanh(gate) * val).astype(BF16)
        acc_ref[...] += jnp.dot(act, wd_ref[c], preferred_element_type=F32)

    u_refs = (u0_ref, u1_ref)

    def step(c, par, last=False):
        if not last:
            up(c + 1, u_refs[1 - par])
        down(c, u_refs[par])

    assert UNROLL % 2 == 0
    up(0, u0_ref)

    def body(i, carry):
        for k in range(UNROLL):
            step(UNROLL * i + k, k % 2)
        return carry

    n_loop = (NCH - 1) // UNROLL
    lax.fori_loop(0, n_loop, body, 0)
    for c in range(n_loop * UNROLL, NCH):
        step(c, c % 2, last=(c == NCH - 1))

    h2 = acc_ref[...]
    gate = _sigmoid(jnp.dot(_rms(h2, gple_ref[...]).astype(BF16), wpg_ref[...], preferred_element_type=F32))
    emb = jnp.dot(p_ref[...].astype(BF16), wpp_ref[...], preferred_element_type=F32)
    h3 = h2 + emb * gate
    out_ref[...] = _rms(h3, gfin_ref[...])


def _mixer(x, mh, ah, p, weights):
    B, S, D = x.shape
    tm = TM_FF
    tok = lambda w: pl.BlockSpec((None, tm, w), lambda b, i: (b, i, 0))
    return pl.pallas_call(
        _mixer_kernel,
        grid=(B, S // tm),
        in_specs=[tok(D), tok(M_WIDTH), tok(A_WIDTH), tok(D_PLE)] + [_const_spec(w.shape) for w in weights],
        out_specs=tok(D),
        out_shape=jax.ShapeDtypeStruct((B, S, D), F32),
        scratch_shapes=[pltpu.VMEM((tm, D), BF16),
                        pltpu.VMEM((tm, D), F32),
                        pltpu.VMEM((2, tm + SUBLANES, FC), F32),
                        pltpu.VMEM((2, tm + SUBLANES, FC), F32),
                        pltpu.VMEM((NCH, 2, SUBLANES, FC), F32)],
        compiler_params=pltpu.CompilerParams(
            dimension_semantics=("arbitrary", "arbitrary"), vmem_limit_bytes=VMEM_LIMIT),
        name="mixer",
    )(x, mh, ah, p, *weights)


def _layer(h, p, g_mix, w_in, b_igate, b_fgate, w_qk_conv, b_qk_conv, g_mhead, rel_bias, w_out,
           g_ffn, w_ffn_up, w_ffn_conv, b_ffn_conv, w_ffn_down, g_ple, w_ple_gate, w_ple_proj, g_out):
    B, S, D = h.shape
    row = lambda v: v.reshape(1, -1).astype(F32)
    m_v = 2 * M_WIDTH
    m_i = 4 * M_WIDTH
    a_q = m_i + 2 * M_HEADS
    wqk = w_in[:, :m_v].astype(BF16)
    wvot = w_in[:, m_v:m_i].T.astype(BF16)
    wgate = w_in[:, m_i:a_q]
    wg = jnp.pad(wgate, ((0, 0), (0, LANES - 2 * M_HEADS))).astype(BF16)
    wgt = wgate.T.astype(BF16)
    watt = w_in[:, a_q:].astype(BF16)
    gbias = jnp.concatenate([b_igate, b_fgate]).astype(F32)
    gb = jnp.pad(gbias, (0, LANES - 2 * M_HEADS)).reshape(1, LANES)
    gbt = gbias.reshape(2 * M_HEADS, 1)

    mq, mk, mvt, mot, gcol, grow, aq, ak, av = _inproj(
        h, row(g_mix), wqk, wvot, wg, wgt, watt, w_qk_conv.astype(F32), row(b_qk_conv), gb, gbt)

    g_mhead_rep = jnp.broadcast_to(g_mhead.astype(F32)[:, None], (M_WIDTH, LANES))
    mh = _mlstm(mq, mk, mvt, mot, gcol, grow, g_mhead_rep)
    ah = _attention(aq, ak, av, rel_bias)

    weights = (
        w_out[:M_WIDTH].astype(BF16), w_out[M_WIDTH:].astype(BF16), row(g_ffn),
        w_ffn_up.astype(BF16), w_ffn_conv.astype(F32), row(b_ffn_conv),
        w_ffn_down.astype(BF16).reshape(NCH, FC, D),
        row(g_ple), w_ple_gate.astype(BF16), w_ple_proj.astype(BF16), g_out,
    )
    return _mixer(h, mh, ah, p, weights)


def kernel(x, p, g_mix, w_in, b_igate, b_fgate, w_qk_conv, b_qk_conv, g_mhead, rel_bias, w_out, g_ffn,
           w_ffn_up, w_ffn_conv, b_ffn_conv, w_ffn_down, g_ple, w_ple_gate, w_ple_proj, g_final):
    depth = w_in.shape[0]
    assert depth == 1, "the final norm is fused into the last layer's channel-mixer kernel"
    i = 0
    return _layer(x, p[i], g_mix[i], w_in[i], b_igate[i], b_fgate[i], w_qk_conv[i], b_qk_conv[i],
                  g_mhead[i], rel_bias[i], w_out[i], g_ffn[i], w_ffn_up[i], w_ffn_conv[i], b_ffn_conv[i],
                  w_ffn_down[i], g_ple[i], w_ple_gate[i], w_ple_proj[i], g_final.reshape(1, -1).astype(F32))
```

```python
import functools

import numpy as np
import jax
import jax.numpy as jnp
from jax import lax
from jax.experimental import pallas as pl
from jax.experimental.pallas import tpu as pltpu

F32 = jnp.float32
BF16 = jnp.bfloat16

D_MODEL = 1024
CHUNK = 64
M_HEADS = 4
M_WIDTH = 512
M_HEAD_DIM = 128
QK_CONV = 4
A_HEADS = 8
A_WIDTH = 512
A_HEAD_DIM = 64
BAND_CHUNKS = 8
MAX_REL = 128
D_FF = 2816
FFN_CONV = 3
D_PLE = 256
EPS = 1e-6

LANES = 128
SUBLANES = 8
NEG = -1e30
LOG2E = 1.4426950408889634

TM_IN = 1024
ML_CHUNK = 128
ML_BLOCK = 2048
TQ = 256
NKB = 1 + (BAND_CHUNKS * CHUNK) // TQ
HG = 2 * LANES // A_HEAD_DIM
Q_SUB = 4
BQ = Q_SUB * TQ
NKBB = 1 + -(-(NKB - 1) * TQ // BQ)
TM_FF = 512
FC = 256
NCH = D_FF // FC
UNROLL = 2

VMEM_LIMIT = 56 * 1024 * 1024

_NT = (((1,), (1,)), ((), ()))


def _rms(x, g):
    return x * lax.rsqrt(jnp.mean(x * x, axis=-1, keepdims=True) + EPS) * g


def _sigmoid(x):
    return 1.0 / (1.0 + jnp.exp(-x))


def _log_sigmoid(x):
    return jnp.minimum(x, 0.0) - jnp.log1p(jnp.exp(-jnp.abs(x)))


def _split3(x):
    hi = x.astype(BF16)
    r = x - hi.astype(F32)
    mid = r.astype(BF16)
    lo = (r - mid.astype(F32)).astype(BF16)
    return hi, mid, lo


def _const_spec(shape):
    n = len(shape)
    return pl.BlockSpec(shape, lambda *_: (0,) * n, pipeline_mode=pl.Buffered(1))


def _inproj_kernel(x_ref, g_ref, wqk_ref, wvot_ref, wg_ref, wgt_ref, watt_ref, cw_ref, cb_ref,
                   gb_ref, gbt_ref,
                   mq_ref, mk_ref, mvt_ref, mot_ref, gcol_ref, grow_ref, aq_ref, ak_ref, av_ref,
                   zext_ref):
    tm = x_ref.shape[0]

    @pl.when(pl.program_id(1) == 0)
    def _():
        zext_ref[0:SUBLANES, :] = jnp.zeros((SUBLANES, 2 * M_WIDTH), F32)

    a = _rms(x_ref[...], g_ref[...]).astype(BF16)

    zqk = jnp.dot(a, wqk_ref[...], preferred_element_type=F32)
    zext_ref[SUBLANES:, :] = zqk
    cw = cw_ref[...]
    acc = cb_ref[...] + cw[QK_CONV - 1:QK_CONV, :] * zqk
    for k in range(1, QK_CONV):
        acc = acc + cw[QK_CONV - 1 - k:QK_CONV - k, :] * zext_ref[SUBLANES - k:SUBLANES - k + tm, :]
    zext_ref[0:SUBLANES, :] = zext_ref[tm:tm + SUBLANES, :]
    qk = acc * _sigmoid(acc)
    mq16 = qk[:, :M_WIDTH].astype(BF16)
    mk16 = (qk[:, M_WIDTH:] * (M_HEAD_DIM ** -0.5)).astype(BF16)
    mq_ref[...] = mq16
    mk_ref[...] = mk16

    zvot = lax.dot_general(wvot_ref[...], a, _NT, preferred_element_type=F32)
    mvt_ref[...] = zvot[:M_WIDTH].astype(BF16)
    mot_ref[...] = _sigmoid(zvot[M_WIDTH:])

    zatt = jnp.dot(a, watt_ref[...], preferred_element_type=F32)
    aq_ref[...] = (zatt[:, :A_WIDTH] * (LOG2E * A_HEAD_DIM ** -0.5)).astype(BF16)
    ak_ref[...] = zatt[:, A_WIDTH:2 * A_WIDTH].astype(BF16)
    av_ref[...] = zatt[:, 2 * A_WIDTH:].astype(BF16)

    zg = jnp.dot(a, wg_ref[...], preferred_element_type=F32) + gb_ref[...]
    lane = lax.broadcasted_iota(jnp.int32, zg.shape, 1)
    half = M_WIDTH // 2
    folded = mq16[:, :half] + mq16[:, half:] + mk16[:, :half] + mk16[:, half:]
    anchor = jnp.dot(folded, wg_ref[0:half, :], preferred_element_type=F32)
    zero = pltpu.bitcast((pltpu.bitcast(anchor, jnp.uint32) >> 16) >> 16, F32)
    gcol_ref[...] = jnp.where(lane < M_HEADS, zg, _log_sigmoid(zg)) + zero
    zgt = lax.dot_general(wgt_ref[...], a, _NT, preferred_element_type=F32) + gbt_ref[...]
    row = lax.broadcasted_iota(jnp.int32, zgt.shape, 0)
    grow_ref[...] = jnp.where(row < M_HEADS, zgt, _log_sigmoid(zgt))


def _inproj(x, g_mix, wqk, wvot, wg, wgt, watt, cw, cb, gb, gbt):
    B, S, D = x.shape
    tm = TM_IN
    tok = lambda w: pl.BlockSpec((None, tm, w), lambda b, i: (b, i, 0))
    feat = lambda w: pl.BlockSpec((None, w, tm), lambda b, i: (b, 0, i))
    out_shape = (
        jax.ShapeDtypeStruct((B, S, M_WIDTH), BF16),
        jax.ShapeDtypeStruct((B, S, M_WIDTH), BF16),
        jax.ShapeDtypeStruct((B, M_WIDTH, S), BF16),
        jax.ShapeDtypeStruct((B, M_WIDTH, S), F32),
        jax.ShapeDtypeStruct((B, S, LANES), F32),
        jax.ShapeDtypeStruct((B, SUBLANES, S), F32),
        jax.ShapeDtypeStruct((B, S, A_WIDTH), BF16),
        jax.ShapeDtypeStruct((B, S, A_WIDTH), BF16),
        jax.ShapeDtypeStruct((B, S, A_WIDTH), BF16),
    )
    out_specs = (tok(M_WIDTH), tok(M_WIDTH), feat(M_WIDTH), feat(M_WIDTH), tok(LANES), feat(SUBLANES),
                 tok(A_WIDTH), tok(A_WIDTH), tok(A_WIDTH))
    consts = (g_mix, wqk, wvot, wg, wgt, watt, cw, cb, gb, gbt)
    return pl.pallas_call(
        _inproj_kernel,
        grid=(B, S // tm),
        in_specs=[tok(D)] + [_const_spec(w.shape) for w in consts],
        out_specs=out_specs,
        out_shape=out_shape,
        scratch_shapes=[pltpu.VMEM((tm + SUBLANES, 2 * M_WIDTH), F32)],
        compiler_params=pltpu.CompilerParams(
            dimension_semantics=("arbitrary", "arbitrary"), vmem_limit_bytes=VMEM_LIMIT),
        name="inproj",
    )(x, *consts)


def _mlstm_consts():
    L = ML_CHUNK
    s = np.arange(L)
    lower = (s[:, None] >= s[None, :]).astype(np.float32)
    upper = lower.T
    sel = np.zeros((LANES, M_HEADS * LANES), np.float32)
    for h in range(M_HEADS):
        sel[h, h * LANES:(h + 1) * LANES] = 1.0
        sel[M_HEADS + h, h * LANES:(h + 1) * LANES] = -1.0
    to_bf16 = lambda m: jnp.asarray(m, BF16)
    return (to_bf16(np.concatenate([lower] * 3, axis=1)),
            to_bf16(np.concatenate([upper] * 3, axis=0)),
            to_bf16(np.concatenate([sel] * 3, axis=0)))


def _mlstm_kernel(q_ref, k_ref, vt_ref, ot_ref, gcol_ref, grow_ref, lower3_ref, upper3_ref, sel3_ref, gh_ref,
                  out_ref, c_ref, n_ref, m_ref):
    L = ML_CHUNK
    dh = M_HEAD_DIM

    @pl.when(pl.program_id(1) == 0)
    def _():
        c_ref[...] = jnp.zeros(c_ref.shape, F32)
        n_ref[...] = jnp.zeros(n_ref.shape, F32)
        m_ref[...] = jnp.zeros(m_ref.shape, F32)

    rr = lax.broadcasted_iota(jnp.int32, (L, L), 0)
    cc = lax.broadcasted_iota(jnp.int32, (L, L), 1)
    causal_t = rr <= cc
    lane = lax.broadcasted_iota(jnp.int32, (L, LANES), 1)
    PR = 2 * SUBLANES

    def blockdiag(x, y):
        zx = jnp.zeros((x.shape[0], y.shape[1]), x.dtype)
        zy = jnp.zeros((y.shape[0], x.shape[1]), y.dtype)
        return jnp.concatenate([jnp.concatenate([x, zx], axis=1), jnp.concatenate([zy, y], axis=1)], axis=0)

    for c in range(q_ref.shape[0] // L):
        rows = slice(c * L, (c + 1) * L)
        gc = gcol_ref[rows, :]
        gr = grow_ref[:, rows]
        bc = jnp.dot(lower3_ref[...], jnp.concatenate(_split3(gc), axis=0), preferred_element_type=F32)
        br = jnp.dot(jnp.concatenate(_split3(gr), axis=1), upper3_ref[...], preferred_element_type=F32)
        ib = jnp.where(lane < M_HEADS, gc, bc)
        a_all = jnp.dot(jnp.concatenate(_split3(ib), axis=1), sel3_ref[...], preferred_element_type=F32)
        for g in range(M_HEADS // 2):
            heads = (2 * g, 2 * g + 1)
            ps = slice(2 * g * dh, (2 * g + 2) * dh)
            qp = q_ref[rows, ps]
            kp = k_ref[rows, ps]
            vtp = vt_ref[ps, rows]
            c_prev = [c_ref[h] for h in heads]
            n_prev = [n_ref[h] for h in heads]
            m_prev = [m_ref[h][0:1, 0:1] for h in heads]

            st = lax.dot_general(kp, blockdiag(qp[:, :dh], qp[:, dh:]), _NT, preferred_element_type=F32)
            n2 = [jnp.concatenate([n, n], axis=0).astype(BF16) for n in n_prev]
            cq = lax.dot_general(
                jnp.concatenate([blockdiag(c_prev[0].astype(BF16), c_prev[1].astype(BF16)),
                                 blockdiag(n2[0], n2[1])], axis=0),
                qp, _NT, preferred_element_type=F32)

            rhs, keep = [], []
            for e, h in enumerate(heads):
                a_c = a_all[:, h * dh:(h + 1) * dh]
                i_r = gr[h:h + 1, :]
                b_r = br[M_HEADS + h:M_HEADS + h + 1, :]
                b_last = b_r[:, L - 1:L]
                dmat = jnp.where(causal_t, a_c + b_r, -jnp.inf)
                inter = b_r + m_prev[e]
                m_t = jnp.maximum(inter, jnp.max(dmat, axis=0, keepdims=True))
                sw = st[:, e * L:(e + 1) * L] * jnp.exp(dmat - m_t)
                g_r = b_last - b_r + i_r
                m_new = jnp.maximum(b_last + m_prev[e], jnp.max(g_r, axis=-1, keepdims=True))
                w_c = jnp.exp(a_c + (b_last - m_new))
                kw = kp[:, e * dh:(e + 1) * dh].astype(F32) * w_c
                rhs.append(jnp.concatenate([sw.astype(BF16), kw.astype(BF16)], axis=1))
                keep.append((m_t, jnp.exp(inter - m_t), jnp.exp(b_last + m_prev[e] - m_new), m_new))

            ones = jnp.ones((PR, L), BF16)
            vs = jnp.dot(
                jnp.concatenate([blockdiag(vtp[:dh], vtp[dh:]), blockdiag(ones, ones)], axis=0),
                jnp.concatenate(rhs, axis=0), preferred_element_type=F32)

            for e, h in enumerate(heads):
                hs = slice(h * dh, (h + 1) * dh)
                m_t, w_inter, decay, m_new = keep[e]
                vr = slice(e * dh, (e + 1) * dh)
                nr = 2 * dh + e * PR
                num = vs[vr, :L] + w_inter * cq[vr, :]
                den = vs[nr:nr + 1, :L] + w_inter * cq[nr:nr + 1, :]
                ht = num * (1.0 / jnp.maximum(jnp.abs(den), jnp.exp(-m_t)))

                mu = jnp.mean(ht, axis=0, keepdims=True)
                dc = ht - mu
                var = jnp.mean(dc * dc, axis=0, keepdims=True)
                yt = dc * lax.rsqrt(var + EPS) * gh_ref[hs, :] * ot_ref[hs, rows]
                out_ref[rows, hs] = yt.T.astype(out_ref.dtype)

                c_ref[h] = decay * c_prev[e] + vs[vr, L:]
                n_ref[h] = decay * n_prev[e] + vs[nr:nr + SUBLANES, L:]
                m_ref[h] = jnp.broadcast_to(m_new, (SUBLANES, LANES))


def _mlstm(mq, mk, mvt, mot, gcol, grow, g_mhead_rep):
    B, S, W = mq.shape
    tb = ML_BLOCK
    tok = lambda w: pl.BlockSpec((None, tb, w), lambda b, j: (b, j, 0))
    feat = lambda w: pl.BlockSpec((None, w, tb), lambda b, j: (b, 0, j))
    consts = _mlstm_consts() + (g_mhead_rep,)
    return pl.pallas_call(
        _mlstm_kernel,
        grid=(B, S // tb),
        in_specs=[tok(W), tok(W), feat(W), feat(W), tok(LANES), feat(SUBLANES)]
        + [_const_spec(w.shape) for w in consts],
        out_specs=tok(W),
        out_shape=jax.ShapeDtypeStruct((B, S, W), BF16),
        scratch_shapes=[pltpu.VMEM((M_HEADS, M_HEAD_DIM, M_HEAD_DIM), F32),
                        pltpu.VMEM((M_HEADS, SUBLANES, M_HEAD_DIM), F32),
                        pltpu.VMEM((M_HEADS, SUBLANES, LANES), F32)],
        compiler_params=pltpu.CompilerParams(
            dimension_semantics=("arbitrary", "arbitrary"), vmem_limit_bytes=VMEM_LIMIT),
        name="mlstm",
    )(mq, mk, mvt, mot, gcol, grow, *consts)


def _tile_delta(qi, ki):
    return qi - ki + (NKB - 1) * TQ // LANES


def _partial_deltas():
    T = LANES
    deltas = sorted({_tile_delta(qi, ki) for qi in range(TQ // T) for ki in range(NKB * TQ // T)})
    return [d for d in deltas if T * d - (T - 1) < MAX_REL and T * d + (T - 1) > -MAX_REL]


def _bias_rows(rel_bias):
    T = LANES
    x = np.arange(2 * T)
    xs = np.where(x < T, x, x - 2 * T)
    rows = []
    for d in _partial_deltas():
        idx = np.clip(T * d - xs, -MAX_REL, MAX_REL) + MAX_REL
        rows.append(rel_bias[:, idx])
    return jnp.stack(rows, axis=1).astype(F32)


def _build_band_bias(rb_ref, rows_ref, bias_ref):
    T = LANES
    rr = lax.broadcasted_iota(jnp.int32, (T, T), 0)
    cc = lax.broadcasted_iota(jnp.int32, (T, T), 1)
    rr2 = lax.broadcasted_iota(jnp.int32, (T, 2 * T), 0)
    back = (NKB - 1) * TQ
    partial = _partial_deltas()
    for h in range(A_HEADS):
        toeplitz = {}
        for n, delta in enumerate(partial):
            tab = jnp.broadcast_to(rows_ref[h, n:n + 1, :], (T, 2 * T))
            for bit in range(T.bit_length() - 1):
                rolled = pltpu.roll(tab, 1 << bit, 1)
                tab = jnp.where(((rr2 >> bit) & 1) == 1, rolled, tab)
            toeplitz[delta] = tab[:, :T]

        for qi in range(TQ // T):
            for ki in range(NKB * TQ // T):
                qc = (rr + qi * T + back) // CHUNK
                kc = (cc + ki * T) // CHUNK
                visible = (kc <= qc) & (kc >= qc - BAND_CHUNKS)
                q_lo, q_hi = (qi * T + back) // CHUNK, (qi * T + T - 1 + back) // CHUNK
                k_lo, k_hi = (ki * T) // CHUNK, (ki * T + T - 1) // CHUNK
                delta = _tile_delta(qi, ki)
                if k_lo > q_hi or k_hi < q_lo - BAND_CHUNKS:
                    tile = jnp.full((T, T), NEG, F32)
                else:
                    if delta in toeplitz:
                        vals = toeplitz[delta]
                    else:
                        far = 2 * MAX_REL if delta > 0 else 0
                        vals = jnp.full((T, T), rb_ref[h, far], F32)
                    tile = jnp.where(visible, vals * LOG2E, NEG)
                bias_ref[h, qi * T:(qi + 1) * T, ki * T:(ki + 1) * T] = tile


def _attn_kernel(rb_ref, rows_ref, q_ref, *refs):
    k_refs = refs[:NKBB]
    v_refs = refs[NKBB:2 * NKBB]
    out_ref = refs[2 * NKBB]
    bias_ref = refs[2 * NKBB + 1]
    j = pl.program_id(1)
    nk = NKB * TQ
    back = (NKB - 1) * TQ
    first = (NKBB - 1) * BQ - back

    def window(blocks, u, gs):
        lo, hi = first + u * TQ, first + u * TQ + nk
        pieces = []
        for n, r in enumerate(blocks):
            a, b = max(lo, n * BQ), min(hi, (n + 1) * BQ)
            if a < b:
                pieces.append(r[a - n * BQ:b - n * BQ, gs])
        return jnp.concatenate(pieces, axis=0)

    @pl.when((pl.program_id(0) == 0) & (j == 0))
    def _():
        _build_band_bias(rb_ref, rows_ref, bias_ref)

    def heads(at_start):
        for u in range(Q_SUB):
            query_block(u, at_start)

    def query_block(u, at_start):
        qs = slice(u * TQ, (u + 1) * TQ)
        if at_start:
            kpos = lax.broadcasted_iota(jnp.int32, (1, nk), 1)
            valid = kpos >= back - u * TQ - j * BQ
        gw = HG * A_HEAD_DIM
        lane_head = lax.broadcasted_iota(jnp.int32, (1, gw), 1) // A_HEAD_DIM
        onehot = [(lane_head == e).astype(BF16) for e in range(HG)]
        for g in range(A_HEADS // HG):
            gs = slice(g * gw, (g + 1) * gw)
            kq = window(k_refs, u, gs)
            vq = window(v_refs, u, gs)
            kbd = jnp.concatenate([kq * onehot[e] for e in range(HG)], axis=0)
            vbd = jnp.concatenate([vq * onehot[e] for e in range(HG)], axis=0)
            s = lax.dot_general(q_ref[qs, gs], kbd, _NT, preferred_element_type=F32)
            probs, scale = [], None
            for e in range(HG):
                sh = s[:, e * nk:(e + 1) * nk] + bias_ref[g * HG + e]
                if at_start:
                    sh = jnp.where(valid, sh, NEG)
                m = jnp.max(sh, axis=-1, keepdims=True)
                p = jnp.exp2(sh - m)
                rl = 1.0 / jnp.sum(p, axis=-1, keepdims=True)
                probs.append(p.astype(BF16))
                scale = rl if e == 0 else jnp.where(lane_head >= e, rl, scale)
            o = jnp.dot(jnp.concatenate(probs, axis=1), vbd, preferred_element_type=F32) * scale
            out_ref[qs, gs] = o.astype(out_ref.dtype)

    pl.when(j * BQ < back)(functools.partial(heads, True))
    pl.when(j * BQ >= back)(functools.partial(heads, False))


def _attention(aq, ak, av, rel_bias):
    B, S, W = aq.shape
    blk = lambda back: pl.BlockSpec((None, BQ, W), lambda b, j: (b, jnp.maximum(j - back, 0), 0))
    kv_specs = [blk(NKBB - 1 - n) for n in range(NKBB)]
    rows = _bias_rows(rel_bias)
    return pl.pallas_call(
        _attn_kernel,
        grid=(B, S // BQ),
        in_specs=[pl.BlockSpec(memory_space=pltpu.SMEM), _const_spec(rows.shape), blk(0)] + kv_specs + kv_specs,
        out_specs=blk(0),
        out_shape=jax.ShapeDtypeStruct((B, S, W), BF16),
        scratch_shapes=[pltpu.VMEM((A_HEADS, TQ, NKB * TQ), F32)],
        compiler_params=pltpu.CompilerParams(
            dimension_semantics=("arbitrary", "arbitrary"), vmem_limit_bytes=VMEM_LIMIT),
        name="band_attn",
    )(rel_bias.astype(F32), rows, aq, *([ak] * NKBB), *([av] * NKBB))


def _gelu_tanh(x):
    c = 0.7978845608028654
    half = 0.5 * x
    return half * jnp.tanh(x * (c + (c * 0.044715) * (x * x))) + half


def _mixer_kernel(x_ref, mh_ref, ah_ref, p_ref, woa_ref, wob_ref, gffn_ref, wu_ref, cw_ref, cb_ref,
                  wd_ref, gple_ref, wpg_ref, wpp_ref, gfin_ref,
                  out_ref, a_ref, acc_ref, u0_ref, u1_ref, carry_ref):
    tm = x_ref.shape[0]

    @pl.when(pl.program_id(1) == 0)
    def _():
        carry_ref[...] = jnp.zeros(carry_ref.shape, F32)

    for r0 in range(0, tm, tm // 2):
        rs = slice(r0, r0 + tm // 2)
        h1 = (x_ref[rs, :]
              + jnp.dot(mh_ref[rs, :], woa_ref[...], preferred_element_type=F32)
              + jnp.dot(ah_ref[rs, :], wob_ref[...], preferred_element_type=F32))
        acc_ref[rs, :] = h1
        a_ref[rs, :] = _rms(h1, gffn_ref[...]).astype(BF16)

    def cols(n):
        if isinstance(n, int):
            return slice(n * FC, (n + 1) * FC)
        return pl.ds(pl.multiple_of(n * FC, FC), FC)

    def up(c, u_ref):
        a = a_ref[...]
        u_ref[0, 0:SUBLANES, :] = carry_ref[c, 0]
        u_ref[1, 0:SUBLANES, :] = carry_ref[c, 1]
        u_ref[0, SUBLANES:, :] = jnp.dot(a, wu_ref[:, cols(c)], preferred_element_type=F32)
        u_ref[1, SUBLANES:, :] = jnp.dot(a, wu_ref[:, cols(NCH + c)], preferred_element_type=F32)

    def conv(u_ref, part, w, b):
        y = b + w[FFN_CONV - 1:FFN_CONV, :] * u_ref[part, SUBLANES:, :]
        for k in range(1, FFN_CONV):
            y = y + w[FFN_CONV - 1 - k:FFN_CONV - k, :] * u_ref[part, SUBLANES - k:SUBLANES - k + tm, :]
        return y

    def down(c, u_ref):
        gate = conv(u_ref, 0, cw_ref[:, cols(c)], cb_ref[:, cols(c)])
        val = conv(u_ref, 1, cw_ref[:, cols(NCH + c)], cb_ref[:, cols(NCH + c)])
        carry_ref[c, 0] = u_ref[0, tm:tm + SUBLANES, :]
        carry_ref[c, 1] = u_ref[1, tm:tm + SUBLANES, :]
        act = (_gelu_tanh(gate) * val).astype(BF16)
        acc_ref[...] += jnp.dot(act, wd_ref[c], preferred_element_type=F32)

    u_refs = (u0_ref, u1_ref)

    def step(c, par, last=False):
        if not last:
            up(c + 1, u_refs[1 - par])
        down(c, u_refs[par])

    assert UNROLL % 2 == 0
    up(0, u0_ref)

    def body(i, carry):
        for k in range(UNROLL):
            step(UNROLL * i + k, k % 2)
        return carry

    n_loop = (NCH - 1) // UNROLL
    lax.fori_loop(0, n_loop, body, 0)
    for c in range(n_loop * UNROLL, NCH):
        step(c, c % 2, last=(c == NCH - 1))

    for r0 in range(0, tm, tm // 2):
        rs = slice(r0, r0 + tm // 2)
        h2 = acc_ref[rs, :]
        gate = _sigmoid(jnp.dot(_rms(h2, gple_ref[...]).astype(BF16), wpg_ref[...], preferred_element_type=F32))
        emb = jnp.dot(p_ref[rs, :].astype(BF16), wpp_ref[...], preferred_element_type=F32)
        h3 = h2 + emb * gate
        out_ref[rs, :] = _rms(h3, gfin_ref[...])


def _mixer(x, mh, ah, p, weights):
    B, S, D = x.shape
    tm = TM_FF
    tok = lambda w: pl.BlockSpec((None, tm, w), lambda b, i: (b, i, 0))
    return pl.pallas_call(
        _mixer_kernel,
        grid=(B, S // tm),
        in_specs=[tok(D), tok(M_WIDTH), tok(A_WIDTH), tok(D_PLE)] + [_const_spec(w.shape) for w in weights],
        out_specs=tok(D),
        out_shape=jax.ShapeDtypeStruct((B, S, D), F32),
        scratch_shapes=[pltpu.VMEM((tm, D), BF16),
                        pltpu.VMEM((tm, D), F32),
                        pltpu.VMEM((2, tm + SUBLANES, FC), F32),
                        pltpu.VMEM((2, tm + SUBLANES, FC), F32),
                        pltpu.VMEM((NCH, 2, SUBLANES, FC), F32)],
        compiler_params=pltpu.CompilerParams(
            dimension_semantics=("arbitrary", "arbitrary"), vmem_limit_bytes=VMEM_LIMIT),
        name="mixer",
    )(x, mh, ah, p, *weights)


def _layer(h, p, g_mix, w_in, b_igate, b_fgate, w_qk_conv, b_qk_conv, g_mhead, rel_bias, w_out,
           g_ffn, w_ffn_up, w_ffn_conv, b_ffn_conv, w_ffn_down, g_ple, w_ple_gate, w_ple_proj, g_out):
    B, S, D = h.shape
    row = lambda v: v.reshape(1, -1).astype(F32)
    m_v = 2 * M_WIDTH
    m_i = 4 * M_WIDTH
    a_q = m_i + 2 * M_HEADS
    wqk = w_in[:, :m_v].astype(BF16)
    wvot = w_in[:, m_v:m_i].T.astype(BF16)
    wgate = w_in[:, m_i:a_q]
    wg = jnp.pad(wgate, ((0, 0), (0, LANES - 2 * M_HEADS))).astype(BF16)
    wgt = wgate.T.astype(BF16)
    watt = w_in[:, a_q:].astype(BF16)
    gbias = jnp.concatenate([b_igate, b_fgate]).astype(F32)
    gb = jnp.pad(gbias, (0, LANES - 2 * M_HEADS)).reshape(1, LANES)
    gbt = gbias.reshape(2 * M_HEADS, 1)

    mq, mk, mvt, mot, gcol, grow, aq, ak, av = _inproj(
        h, row(g_mix), wqk, wvot, wg, wgt, watt, w_qk_conv.astype(F32), row(b_qk_conv), gb, gbt)

    g_mhead_rep = jnp.broadcast_to(g_mhead.astype(F32)[:, None], (M_WIDTH, LANES))
    mh = _mlstm(mq, mk, mvt, mot, gcol, grow, g_mhead_rep)
    ah = _attention(aq, ak, av, rel_bias)

    weights = (
        w_out[:M_WIDTH].astype(BF16), w_out[M_WIDTH:].astype(BF16), row(g_ffn),
        w_ffn_up.astype(BF16), w_ffn_conv.astype(F32), row(b_ffn_conv),
        w_ffn_down.astype(BF16).reshape(NCH, FC, D),
        row(g_ple), w_ple_gate.astype(BF16), w_ple_proj.astype(BF16), g_out,
    )
    return _mixer(h, mh, ah, p, weights)


def kernel(x, p, g_mix, w_in, b_igate, b_fgate, w_qk_conv, b_qk_conv, g_mhead, rel_bias, w_out, g_ffn,
           w_ffn_up, w_ffn_conv, b_ffn_conv, w_ffn_down, g_ple, w_ple_gate, w_ple_proj, g_final):
    depth = w_in.shape[0]
    assert depth == 1, "the final norm is fused into the last layer's channel-mixer kernel"
    i = 0
    return _layer(x, p[i], g_mix[i], w_in[i], b_igate[i], b_fgate[i], w_qk_conv[i], b_qk_conv[i],
                  g_mhead[i], rel_bias[i], w_out[i], g_ffn[i], w_ffn_up[i], w_ffn_conv[i], b_ffn_conv[i],
                  w_ffn_down[i], g_ple[i], w_ple_gate[i], w_ple_proj[i], g_final.reshape(1, -1).astype(F32))
```
